```python
import jax, jax.numpy as jnp
from jax import lax
import numpy as np

D_MODEL = 4096
BATCH = 2
SEQ = 8192
DEPTH = 1

D_MIX = D_MODEL
HEAD_DIM = 128
ATTN_WIDTH = D_MIX // 2
N_ATTN_HEADS = ATTN_WIDTH // HEAD_DIM
CONV_WIDTH = D_MIX - ATTN_WIDTH
CONV_GROUP = 128
N_CONV_GROUPS = CONV_WIDTH // CONV_GROUP
CONV_K = 3
Q_BLOCK = 128
D_FF = ((8 * D_MODEL // 3 + 255) // 256) * 256
N_MOD = 6
EPS = 1e-6
IN_SPLITS = (ATTN_WIDTH, ATTN_WIDTH, ATTN_WIDTH, N_ATTN_HEADS, CONV_WIDTH, CONV_WIDTH, CONV_WIDTH)
N_IN = sum(IN_SPLITS)

kernel_name = "hymba_conv_fox_sandwich_adaln_block"


def rms_norm(x, g):
    xf = x.astype(jnp.float32)
    y = xf * lax.rsqrt(jnp.mean(xf * xf, axis=-1, keepdims=True) + EPS)
    return (y * g.astype(jnp.float32)).astype(x.dtype)


def modulate(h, shift, scale):
    return h * (1 + scale[:, None, :]) + shift[:, None, :]


def causal_short_conv(u, w):
    s = u.shape[1]
    u_pad = jnp.pad(u, ((0, 0), (CONV_K - 1, 0), (0, 0)))
    return sum(w[k][None, None, :] * u_pad[:, k:k + s, :] for k in range(CONV_K))


def forgetting_attention(q, k, v, f_logit):
    b, s, h, dh = q.shape
    scale = dh ** -0.5
    log_f = jax.nn.log_sigmoid(f_logit.astype(jnp.float32))
    F = jnp.cumsum(log_f, axis=1).transpose(0, 2, 1)
    qh = q.transpose(0, 2, 1, 3)
    kh = k.transpose(0, 2, 1, 3)
    vh = v.transpose(0, 2, 1, 3)
    nb = s // Q_BLOCK
    qb = qh.reshape(b, h, nb, Q_BLOCK, dh).transpose(2, 0, 1, 3, 4)
    Fb = F.reshape(b, h, nb, Q_BLOCK).transpose(2, 0, 1, 3)
    kpos = jnp.arange(s)

    def block(args):
        q_blk, F_blk, i = args
        logits = jnp.einsum('bhqd,bhkd->bhqk', q_blk, kh).astype(jnp.float32) * scale
        logits = logits + F_blk[..., None] - F[:, :, None, :]
        qpos = i * Q_BLOCK + jnp.arange(Q_BLOCK)
        mask = kpos[None, :] <= qpos[:, None]
        logits = jnp.where(mask[None, None], logits, -jnp.inf)
        p = jax.nn.softmax(logits, axis=-1)
        return jnp.einsum('bhqk,bhkd->bhqd', p.astype(vh.dtype), vh)

    out = lax.map(block, (qb, Fb, jnp.arange(nb)))
    return out.transpose(1, 0, 3, 2, 4).reshape(b, s, h * dh)


def hybrid_mixer(h, w_in, b_f, conv_w, attn_out_norm, conv_out_norm, w_out):
    b, s, _ = h.shape
    proj = jnp.einsum('bsd,dn->bsn', h, w_in)
    idx = list(np.cumsum(IN_SPLITS)[:-1])
    q, k, v, f_logit, gate_b, gate_c, u = jnp.split(proj, idx, axis=-1)
    q = q.reshape(b, s, N_ATTN_HEADS, HEAD_DIM)
    k = k.reshape(b, s, N_ATTN_HEADS, HEAD_DIM)
    v = v.reshape(b, s, N_ATTN_HEADS, HEAD_DIM)
    y_attn = forgetting_attention(q, k, v, f_logit + b_f)
    y_conv = gate_b * causal_short_conv(gate_c * u, conv_w)
    y = jnp.concatenate([rms_norm(y_attn, attn_out_norm), rms_norm(y_conv, conv_out_norm)], axis=-1)
    return jnp.einsum('bsm,md->bsd', y, w_out)


def swiglu(h, w_gate, w_up, w_down):
    g = jnp.einsum('bsd,df->bsf', h, w_gate)
    u = jnp.einsum('bsd,df->bsf', h, w_up)
    return jnp.einsum('bsf,fd->bsd', jax.nn.silu(g) * u, w_down)


def setup_inputs(seed: int = 0) -> dict:
    key = jax.random.key(seed)
    ks = jax.random.split(key, 20)
    d = D_MODEL
    L = DEPTH
    nrm = lambda k, shape, s: jax.random.normal(k, shape, jnp.float32) * s
    gain = lambda k, n: 1.0 + nrm(k, (L, n), 0.05)
    return {
        "x": nrm(ks[0], (BATCH, SEQ, d), 1.0),
        "c": nrm(ks[1], (BATCH, d), 1.0),
        "w_ada": nrm(ks[2], (L, d, N_MOD * d), 0.5 * d ** -0.5),
        "b_ada": nrm(ks[3], (L, N_MOD * d), 0.01),
        "pre_norm_mix": gain(ks[4], d),
        "w_in": nrm(ks[5], (L, d, N_IN), d ** -0.5),
        "b_f": jax.random.uniform(ks[6], (L, N_ATTN_HEADS), jnp.float32, 1.0, 4.0),
        "conv_w": nrm(ks[7], (L, CONV_K, CONV_WIDTH), CONV_K ** -0.5),
        "attn_out_norm": gain(ks[8], ATTN_WIDTH),
        "conv_out_norm": gain(ks[9], CONV_WIDTH),
        "w_out": nrm(ks[10], (L, D_MIX, d), D_MIX ** -0.5),
        "post_norm_mix": gain(ks[11], d),
        "pre_norm_ffn": gain(ks[12], d),
        "w_gate": nrm(ks[13], (L, d, D_FF), d ** -0.5),
        "w_up": nrm(ks[14], (L, d, D_FF), d ** -0.5),
        "w_down": nrm(ks[15], (L, D_FF, d), D_FF ** -0.5),
        "post_norm_ffn": gain(ks[16], d),
    }


def reference(x, c, w_ada, b_ada, pre_norm_mix, w_in, b_f, conv_w, attn_out_norm,
              conv_out_norm, w_out, post_norm_mix, pre_norm_ffn, w_gate, w_up, w_down,
              post_norm_ffn):
    c_act = jax.nn.silu(c)
    for l in range(DEPTH):
        mod = jnp.einsum('bd,dm->bm', c_act, w_ada[l]) + b_ada[l]
        sh1, sc1, g1, sh2, sc2, g2 = jnp.split(mod, N_MOD, axis=-1)
        h = modulate(rms_norm(x, pre_norm_mix[l]), sh1, sc1)
        h = hybrid_mixer(h, w_in[l], b_f[l], conv_w[l], attn_out_norm[l], conv_out_norm[l], w_out[l])
        x = x + g1[:, None, :] * rms_norm(h, post_norm_mix[l])
        h = modulate(rms_norm(x, pre_norm_ffn[l]), sh2, sc2)
        h = swiglu(h, w_gate[l], w_up[l], w_down[l])
        x = x + g2[:, None, :] * rms_norm(h, post_norm_ffn[l])
    return x
```

```python
import functools

import jax
import jax.numpy as jnp
from jax import lax
from jax.experimental import pallas as pl
from jax.experimental.pallas import tpu as pltpu

HEAD_DIM = 128
CONV_K = 3
EPS = 1e-6
N_MOD = 6
V7X_VMEM_BYTES = 64 * 1024 * 1024
VMEM_LIMIT = 56 * 1024 * 1024
SUBLANES = 8

F32 = jnp.float32
BF16 = jnp.bfloat16


def _cparams(sem):
    return pltpu.CompilerParams(dimension_semantics=sem, vmem_limit_bytes=VMEM_LIMIT)


def _rms(x, g):
    return x * lax.rsqrt(jnp.mean(x * x, axis=-1, keepdims=True) + EPS) * g


def _ada_kernel(c_ref, w_ref, b_ref, o_ref):
    c = c_ref[...]
    c_act = (c * jax.nn.sigmoid(c)).astype(BF16)
    o_ref[...] = jnp.dot(c_act, w_ref[...].astype(BF16),
                         preferred_element_type=F32) + b_ref[...]


def _ada(c, w, b, bn=512):
    bsz, d = c.shape
    n = w.shape[1]
    return pl.pallas_call(
        _ada_kernel,
        out_shape=jax.ShapeDtypeStruct((bsz, n), F32),
        grid=(n // bn,),
        in_specs=[pl.BlockSpec((bsz, d), lambda j: (0, 0)),
                  pl.BlockSpec((d, bn), lambda j: (0, j)),
                  pl.BlockSpec((1, bn), lambda j: (0, j))],
        out_specs=pl.BlockSpec((bsz, bn), lambda j: (0, j)),
        compiler_params=_cparams(("arbitrary",)),
        name="ada_mod",
    )(c, w, b.reshape(1, n))


def _prenorm_kernel(x_ref, mod_ref, g_ref, o_ref):
    y = _rms(x_ref[...], g_ref[...])
    o_ref[...] = (y * (1.0 + mod_ref[1:2, :]) + mod_ref[0:1, :]).astype(o_ref.dtype)


def _prenorm(x2, mod3, g, seq, bt=512):
    t, d = x2.shape
    return pl.pallas_call(
        _prenorm_kernel,
        out_shape=jax.ShapeDtypeStruct((t, d), BF16),
        grid=(t // bt,),
        in_specs=[pl.BlockSpec((bt, d), lambda i: (i, 0)),
                  pl.BlockSpec((None, N_MOD, d), lambda i: ((i * bt) // seq, 0, 0)),
                  pl.BlockSpec((1, d), lambda i: (0, 0))],
        out_specs=pl.BlockSpec((bt, d), lambda i: (i, 0)),
        compiler_params=_cparams(("arbitrary",)),
        name="prenorm_mod",
    )(x2, mod3, g.reshape(1, d))


def _mm_kernel(x_ref, w_ref, o_ref):
    o_ref[...] = jnp.dot(x_ref[...], w_ref[...],
                         preferred_element_type=F32).astype(o_ref.dtype)


def _mm(x, w, out_dtype, bm, bn, name):
    m, k = x.shape
    n = w.shape[1]
    assert m % bm == 0 and n % bn == 0
    return pl.pallas_call(
        _mm_kernel,
        out_shape=jax.ShapeDtypeStruct((m, n), out_dtype),
        grid=(m // bm, n // bn),
        in_specs=[pl.BlockSpec((bm, k), lambda i, j: (i, 0)),
                  pl.BlockSpec((k, bn), lambda i, j: (0, j))],
        out_specs=pl.BlockSpec((bm, bn), lambda i, j: (i, j)),
        compiler_params=_cparams(("arbitrary", "arbitrary")),
        name=name,
    )(x, w)


def _fcum_kernel(f_ref, b_ref, o_ref, carry_ref, *, bs):
    @pl.when(pl.program_id(1) == 0)
    def _():
        carry_ref[...] = jnp.zeros_like(carry_ref)

    x = f_ref[...] + b_ref[...]
    log_f = jnp.minimum(x, 0.0) - jnp.log1p(jnp.exp(-jnp.abs(x)))
    row = lax.broadcasted_iota(jnp.int32, (bs, bs), 0)
    col = lax.broadcasted_iota(jnp.int32, (bs, bs), 1)
    tril = (col <= row).astype(F32)
    cum = jnp.dot(tril, log_f, preferred_element_type=F32,
                  precision=lax.Precision.HIGHEST) + carry_ref[...]
    o_ref[...] = cum
    carry_ref[...] = cum[bs - 1:bs, :]


def _fcum(f3, b_pad, bs=512):
    bsz, seq, n = f3.shape
    return pl.pallas_call(
        functools.partial(_fcum_kernel, bs=bs),
        out_shape=jax.ShapeDtypeStruct((bsz, seq, n), F32),
        grid=(bsz, seq // bs),
        in_specs=[pl.BlockSpec((None, bs, n), lambda b, i: (b, i, 0)),
                  pl.BlockSpec((1, n), lambda b, i: (0, 0))],
        out_specs=pl.BlockSpec((None, bs, n), lambda b, i: (b, i, 0)),
        scratch_shapes=[pltpu.VMEM((1, n), F32)],
        compiler_params=_cparams(("arbitrary", "arbitrary")),
        name="forget_cumsum",
    )(f3, b_pad)


def _attn_kernel(q_ref, k_ref, v_ref, fq_ref, fk_ref, o_ref,
                 ft_sc, m_sc, l_sc, acc_sc, *, bq, bk, scale):
    h = pl.program_id(1)
    qi = pl.program_id(2)
    lane = lax.broadcasted_iota(jnp.int32, fq_ref.shape, 1)
    ft_sc[...] = jnp.sum(jnp.where(lane == h, fq_ref[...], 0.0), axis=1, keepdims=True)
    m_sc[...] = jnp.full_like(m_sc, -jnp.inf)
    l_sc[...] = jnp.zeros_like(l_sc)
    acc_sc[...] = jnp.zeros_like(acc_sc)
    q = q_ref[...]

    def block(j, masked):
        start = pl.multiple_of(j * bk, bk)
        k = k_ref[pl.ds(start, bk), :]
        v = v_ref[pl.ds(start, bk), :]
        fs = fk_ref[:, pl.ds(start, bk)]
        s = lax.dot_general(q, k, (((1,), (1,)), ((), ())),
                            preferred_element_type=F32) * scale
        logit = s + ft_sc[...] - fs
        if masked:
            qpos = qi * bq + lax.broadcasted_iota(jnp.int32, (bq, bk), 0)
            kpos = start + lax.broadcasted_iota(jnp.int32, (bq, bk), 1)
            logit = jnp.where(kpos <= qpos, logit, -jnp.inf)
        m_prev = m_sc[...]
        m_new = jnp.maximum(m_prev, jnp.max(logit, axis=1, keepdims=True))
        alpha = jnp.exp(m_prev - m_new)
        p = jnp.exp(logit - m_new)
        l_sc[...] = alpha * l_sc[...] + jnp.sum(p, axis=1, keepdims=True)
        acc_sc[...] = alpha * acc_sc[...] + jnp.dot(
            p.astype(v.dtype), v, preferred_element_type=F32)
        m_sc[...] = m_new

    def body(j, carry):
        block(j, False)
        return carry

    lax.fori_loop(0, qi, body, 0)
    block(qi, True)
    o_ref[...] = (acc_sc[...] / l_sc[...]).astype(o_ref.dtype)


def _attention(qkv3, f3, fk4, n_heads, bq=512):
    bsz, seq, _ = qkv3.shape
    bk = bq
    scale = HEAD_DIM ** -0.5
    kern = functools.partial(_attn_kernel, bq=bq, bk=bk, scale=scale)
    return pl.pallas_call(
        kern,
        out_shape=jax.ShapeDtypeStruct((bsz, seq, n_heads * HEAD_DIM), F32),
        grid=(bsz, n_heads, seq // bq),
        in_specs=[
            pl.BlockSpec((None, bq, HEAD_DIM), lambda b, h, i: (b, i, h)),
            pl.BlockSpec((None, seq, HEAD_DIM), lambda b, h, i: (b, 0, n_heads + h)),
            pl.BlockSpec((None, seq, HEAD_DIM), lambda b, h, i: (b, 0, 2 * n_heads + h)),
            pl.BlockSpec((None, bq, f3.shape[2]), lambda b, h, i: (b, i, 0)),
            pl.BlockSpec((None, None, 1, seq), lambda b, h, i: (b, h, 0, 0)),
        ],
        out_specs=pl.BlockSpec((None, bq, HEAD_DIM), lambda b, h, i: (b, i, h)),
        scratch_shapes=[pltpu.VMEM((bq, 1), F32), pltpu.VMEM((bq, 1), F32),
                        pltpu.VMEM((bq, 1), F32), pltpu.VMEM((bq, HEAD_DIM), F32)],
        compiler_params=_cparams(("arbitrary", "arbitrary", "arbitrary")),
        name="fox_attention",
    )(qkv3, qkv3, qkv3, f3, fk4)


def _mixpost_kernel(ya_ref, gb_ref, gc_ref, u_ref, gch_ref, uh_ref, cw_ref,
                    ga_ref, gcn_ref, o_ref, *, bt, seq, width):
    i = pl.program_id(0)
    z = gc_ref[...] * u_ref[...]
    zp = gch_ref[...] * uh_ref[...]
    zp = jnp.where((i * bt) % seq == 0, 0.0, zp)
    row = lax.broadcasted_iota(jnp.int32, (bt, 1), 0)
    z1 = jnp.where(row == 0, zp[SUBLANES - 1:SUBLANES, :], pltpu.roll(z, 1, 0))
    z2 = jnp.where(row == 0, zp[SUBLANES - 2:SUBLANES - 1, :],
                   jnp.where(row == 1, zp[SUBLANES - 1:SUBLANES, :], pltpu.roll(z, 2, 0)))
    conv = cw_ref[2:3, :] * z + cw_ref[1:2, :] * z1 + cw_ref[0:1, :] * z2
    y_conv = gb_ref[...] * conv
    o_ref[:, :width] = _rms(ya_ref[...], ga_ref[...]).astype(o_ref.dtype)
    o_ref[:, width:] = _rms(y_conv, gcn_ref[...]).astype(o_ref.dtype)


def _mixpost(y_attn, gcu, conv_w, g_attn, g_conv, seq, bt=256):
    t, width = y_attn.shape
    hb = bt // SUBLANES
    kern = functools.partial(_mixpost_kernel, bt=bt, seq=seq, width=width)
    halo = lambda c: pl.BlockSpec((SUBLANES, width),
                                  lambda i: (jnp.maximum(i * hb - 1, 0), c))
    return pl.pallas_call(
        kern,
        out_shape=jax.ShapeDtypeStruct((t, 2 * width), BF16),
        grid=(t // bt,),
        in_specs=[pl.BlockSpec((bt, width), lambda i: (i, 0)),
                  pl.BlockSpec((bt, width), lambda i: (i, 0)),
                  pl.BlockSpec((bt, width), lambda i: (i, 1)),
                  pl.BlockSpec((bt, width), lambda i: (i, 2)),
                  halo(1), halo(2),
                  pl.BlockSpec((CONV_K, width), lambda i: (0, 0)),
                  pl.BlockSpec((1, width), lambda i: (0, 0)),
                  pl.BlockSpec((1, width), lambda i: (0, 0))],
        out_specs=pl.BlockSpec((bt, 2 * width), lambda i: (i, 0)),
        compiler_params=_cparams(("arbitrary",)),
        name="conv_gate_norm",
    )(y_attn, gcu, gcu, gcu, gcu, gcu, conv_w,
      g_attn.reshape(1, width), g_conv.reshape(1, width))


def _res1_kernel(x_ref, h_ref, mod_ref, gpost_ref, gpre_ref, x1_ref, h2_ref):
    x1 = x_ref[...] + mod_ref[2:3, :] * _rms(h_ref[...], gpost_ref[...])
    x1_ref[...] = x1
    y = _rms(x1, gpre_ref[...])
    h2_ref[...] = (y * (1.0 + mod_ref[4:5, :]) + mod_ref[3:4, :]).astype(h2_ref.dtype)


def _res1(x2, h, mod3, g_post, g_pre, seq, bt=256):
    t, d = x2.shape
    row = pl.BlockSpec((bt, d), lambda i: (i, 0))
    vec = pl.BlockSpec((1, d), lambda i: (0, 0))
    return pl.pallas_call(
        _res1_kernel,
        out_shape=(jax.ShapeDtypeStruct((t, d), F32), jax.ShapeDtypeStruct((t, d), BF16)),
        grid=(t // bt,),
        in_specs=[row, row,
                  pl.BlockSpec((None, N_MOD, d), lambda i: ((i * bt) // seq, 0, 0)),
                  vec, vec],
        out_specs=(row, row),
        compiler_params=_cparams(("arbitrary",)),
        name="residual_norm_mod",
    )(x2, h, mod3, g_post.reshape(1, d), g_pre.reshape(1, d))


def _res2_kernel(x_ref, h_ref, mod_ref, gpost_ref, o_ref):
    o_ref[...] = x_ref[...] + mod_ref[5:6, :] * _rms(h_ref[...], gpost_ref[...])


def _res2(x1, h, mod3, g_post, seq, bt=256):
    t, d = x1.shape
    row = pl.BlockSpec((bt, d), lambda i: (i, 0))
    return pl.pallas_call(
        _res2_kernel,
        out_shape=jax.ShapeDtypeStruct((t, d), F32),
        grid=(t // bt,),
        in_specs=[row, row,
                  pl.BlockSpec((None, N_MOD, d), lambda i: ((i * bt) // seq, 0, 0)),
                  pl.BlockSpec((1, d), lambda i: (0, 0))],
        out_specs=row,
        compiler_params=_cparams(("arbitrary",)),
        name="residual_norm",
    )(x1, h, mod3, g_post.reshape(1, d))


def _ffn_up_kernel(x_ref, wg_ref, wu_ref, o_ref):
    x = x_ref[...]
    g = jnp.dot(x, wg_ref[...], preferred_element_type=F32)
    u = jnp.dot(x, wu_ref[...], preferred_element_type=F32)
    o_ref[...] = (g * jax.nn.sigmoid(g) * u).astype(o_ref.dtype)


def _ffn_up(x, wg, wu, bm=1024, bn=256):
    m, k = x.shape
    n = wg.shape[1]
    assert m % bm == 0 and n % bn == 0
    wspec = pl.BlockSpec((k, bn), lambda i, j: (0, j))
    return pl.pallas_call(
        _ffn_up_kernel,
        out_shape=jax.ShapeDtypeStruct((m, n), BF16),
        grid=(m // bm, n // bn),
        in_specs=[pl.BlockSpec((bm, k), lambda i, j: (i, 0)), wspec, wspec],
        out_specs=pl.BlockSpec((bm, bn), lambda i, j: (i, j)),
        compiler_params=_cparams(("arbitrary", "arbitrary")),
        name="swiglu_up",
    )(x, wg, wu)


def kernel(x, c, w_ada, b_ada, pre_norm_mix, w_in, b_f, conv_w, attn_out_norm,
           conv_out_norm, w_out, post_norm_mix, pre_norm_ffn, w_gate, w_up, w_down,
           post_norm_ffn):
    bsz, seq, d = x.shape
    depth = w_ada.shape[0]
    n_heads = b_f.shape[1]
    attn_w = n_heads * HEAD_DIM
    conv_wd = conv_w.shape[2]
    t = bsz * seq
    lane = 128

    x2 = x.reshape(t, d)
    for l in range(depth):
        w_qkv = w_in[l][:, :3 * attn_w].astype(BF16)
        w_f = jnp.pad(w_in[l][:, 3 * attn_w:3 * attn_w + n_heads],
                      ((0, 0), (0, lane - n_heads))).astype(BF16)
        w_gcu = w_in[l][:, 3 * attn_w + n_heads:].astype(BF16)
        w_o = w_out[l].astype(BF16)
        w_g = w_gate[l].astype(BF16)
        w_u = w_up[l].astype(BF16)
        w_d = w_down[l].astype(BF16)
        b_pad = jnp.pad(b_f[l], (0, lane - n_heads)).reshape(1, lane)

        mod3 = _ada(c, w_ada[l], b_ada[l]).reshape(bsz, N_MOD, d)

        h1 = _prenorm(x2, mod3, pre_norm_mix[l], seq)
        qkv = _mm(h1, w_qkv, BF16, 1024, 1024, "in_proj_qkv")
        gcu = _mm(h1, w_gcu, F32, 1024, 1024, "in_proj_conv")
        f_logit = _mm(h1, w_f, F32, 1024, lane, "in_proj_forget")
        f_cum = _fcum(f_logit.reshape(bsz, seq, lane), b_pad)
        fk4 = f_cum[:, :, :n_heads].transpose(0, 2, 1).reshape(bsz, n_heads, 1, seq)
        y_attn = _attention(qkv.reshape(bsz, seq, 3 * attn_w), f_cum, fk4, n_heads)
        y_cat = _mixpost(y_attn.reshape(t, attn_w), gcu, conv_w[l],
                         attn_out_norm[l], conv_out_norm[l], seq)
        h = _mm(y_cat, w_o, F32, 1024, 1024, "out_proj")
        x2, h2 = _res1(x2, h, mod3, post_norm_mix[l], pre_norm_ffn[l], seq)

        a = _ffn_up(h2, w_g, w_u)
        h3 = _mm(a, w_d, F32, 512, 256, "swiglu_down")
        x2 = _res2(x2, h3, mod3, post_norm_ffn[l], seq)
    return x2.reshape(bsz, seq, d)
```

```python
import functools

import jax
import jax.numpy as jnp
from jax import lax
from jax.experimental import pallas as pl
from jax.experimental.pallas import tpu as pltpu

HEAD_DIM = 128
CONV_K = 3
EPS = 1e-6
N_MOD = 6
V7X_VMEM_BYTES = 64 * 1024 * 1024
VMEM_LIMIT = 56 * 1024 * 1024
SUBLANES = 8

F32 = jnp.float32
BF16 = jnp.bfloat16


def _cparams(sem):
    return pltpu.CompilerParams(dimension_semantics=sem, vmem_limit_bytes=VMEM_LIMIT)


def _rms(x, g):
    return x * lax.rsqrt(jnp.mean(x * x, axis=-1, keepdims=True) + EPS) * g


def _ada_kernel(c_ref, w_ref, b_ref, o_ref):
    c = c_ref[...]
    c_act = (c * jax.nn.sigmoid(c)).astype(BF16)
    o_ref[...] = jnp.dot(c_act, w_ref[...].astype(BF16),
                         preferred_element_type=F32) + b_ref[...]


def _ada(c, w, b, bn=512):
    bsz, d = c.shape
    n = w.shape[1]
    return pl.pallas_call(
        _ada_kernel,
        out_shape=jax.ShapeDtypeStruct((bsz, n), F32),
        grid=(n // bn,),
        in_specs=[pl.BlockSpec((bsz, d), lambda j: (0, 0)),
                  pl.BlockSpec((d, bn), lambda j: (0, j)),
                  pl.BlockSpec((1, bn), lambda j: (0, j))],
        out_specs=pl.BlockSpec((bsz, bn), lambda j: (0, j)),
        compiler_params=_cparams(("arbitrary",)),
        name="ada_mod",
    )(c, w, b.reshape(1, n))


def _prenorm_kernel(x_ref, mod_ref, g_ref, o_ref):
    y = _rms(x_ref[...], g_ref[...])
    o_ref[...] = (y * (1.0 + mod_ref[1:2, :]) + mod_ref[0:1, :]).astype(o_ref.dtype)


def _prenorm(x2, mod3, g, seq, bt=512):
    t, d = x2.shape
    return pl.pallas_call(
        _prenorm_kernel,
        out_shape=jax.ShapeDtypeStruct((t, d), BF16),
        grid=(t // bt,),
        in_specs=[pl.BlockSpec((bt, d), lambda i: (i, 0)),
                  pl.BlockSpec((None, N_MOD, d), lambda i: ((i * bt) // seq, 0, 0)),
                  pl.BlockSpec((1, d), lambda i: (0, 0))],
        out_specs=pl.BlockSpec((bt, d), lambda i: (i, 0)),
        compiler_params=_cparams(("arbitrary",)),
        name="prenorm_mod",
    )(x2, mod3, g.reshape(1, d))


def _mm_kernel(x_ref, w_ref, o_ref, *, scaled_blocks, scale):
    acc = jnp.dot(x_ref[...], w_ref[...], preferred_element_type=F32)
    if scaled_blocks:
        acc = acc * jnp.where(pl.program_id(1) < scaled_blocks, scale, 1.0)
    o_ref[...] = acc.astype(o_ref.dtype)


def _mm(x, w, out_dtype, bm, bn, name, scaled_cols=0, scale=1.0):
    m, k = x.shape
    n = w.shape[1]
    assert m % bm == 0 and n % bn == 0 and scaled_cols % bn == 0
    return pl.pallas_call(
        functools.partial(_mm_kernel, scaled_blocks=scaled_cols // bn, scale=scale),
        out_shape=jax.ShapeDtypeStruct((m, n), out_dtype),
        grid=(m // bm, n // bn),
        in_specs=[pl.BlockSpec((bm, k), lambda i, j: (i, 0)),
                  pl.BlockSpec((k, bn), lambda i, j: (0, j))],
        out_specs=pl.BlockSpec((bm, bn), lambda i, j: (i, j)),
        compiler_params=_cparams(("arbitrary", "arbitrary")),
        name=name,
    )(x, w)


def _fcum_kernel(f_ref, b_ref, o_ref, carry_ref, *, bs):
    @pl.when(pl.program_id(1) == 0)
    def _():
        carry_ref[...] = jnp.zeros_like(carry_ref)

    x = f_ref[...] + b_ref[...]
    log_f = jnp.minimum(x, 0.0) - jnp.log1p(jnp.exp(-jnp.abs(x)))
    row = lax.broadcasted_iota(jnp.int32, (bs, bs), 0)
    col = lax.broadcasted_iota(jnp.int32, (bs, bs), 1)
    tril = (col <= row).astype(F32)
    cum = jnp.dot(tril, log_f, preferred_element_type=F32,
                  precision=lax.Precision.HIGHEST) + carry_ref[...]
    o_ref[...] = cum
    carry_ref[...] = cum[bs - 1:bs, :]


def _fcum(f3, b_pad, bs=512):
    bsz, seq, n = f3.shape
    return pl.pallas_call(
        functools.partial(_fcum_kernel, bs=bs),
        out_shape=jax.ShapeDtypeStruct((bsz, seq, n), F32),
        grid=(bsz, seq // bs),
        in_specs=[pl.BlockSpec((None, bs, n), lambda b, i: (b, i, 0)),
                  pl.BlockSpec((1, n), lambda b, i: (0, 0))],
        out_specs=pl.BlockSpec((None, bs, n), lambda b, i: (b, i, 0)),
        scratch_shapes=[pltpu.VMEM((1, n), F32)],
        compiler_params=_cparams(("arbitrary", "arbitrary")),
        name="forget_cumsum",
    )(f3, b_pad)


LOG2E = 1.4426950408889634
AUG = 16
N_SPLIT = 3


def _aug_cols(f_col, own_first):
    n = f_col.shape[0]
    hi = f_col.astype(BF16).astype(F32)
    r = f_col - hi
    mid = r.astype(BF16).astype(F32)
    lo = (r - mid).astype(BF16).astype(F32)
    lane = lax.broadcasted_iota(jnp.int32, (n, AUG), 1)
    base = 0 if own_first else N_SPLIT
    other = N_SPLIT - base
    ones = jnp.where((lane >= other) & (lane < other + N_SPLIT), 1.0, 0.0)
    vals = jnp.where(lane == base, hi,
                     jnp.where(lane == base + 1, mid,
                               jnp.where(lane == base + 2, lo, ones)))
    return vals.astype(BF16)


def _head_col(f, h):
    lane = lax.broadcasted_iota(jnp.int32, f.shape, 1)
    return jnp.sum(jnp.where(lane == h, f, 0.0), axis=1, keepdims=True)


def _attn_kernel(q_ref, k_ref, v_ref, fq_ref, fs_ref, o_ref,
                 q2_sc, k2_sc, v2_sc, sa_sc, sb_sc, m_sc, acc_sc, *, bq, bk, seq, chunk):
    h = pl.program_id(1)
    qi = pl.program_id(2)
    d = HEAD_DIM

    @pl.when(qi == 0)
    def _():
        def body(c, carry):
            r = pl.multiple_of(c * chunk, chunk)
            fcol = _head_col(fs_ref[pl.ds(r, chunk), :], h) * (-LOG2E)
            k2_sc[pl.ds(r, chunk), :d] = k_ref[pl.ds(r, chunk), :]
            k2_sc[pl.ds(r, chunk), d:] = _aug_cols(fcol, own_first=False)
            v2_sc[pl.ds(r, chunk), :d] = v_ref[pl.ds(r, chunk), :]
            v2_sc[pl.ds(r, chunk), d:] = jnp.ones((chunk, d), BF16)
            return carry
        lax.fori_loop(0, seq // chunk, body, 0)

    ft = _head_col(fq_ref[...], h) * LOG2E
    q2_sc[:, :d] = q_ref[...]
    q2_sc[:, d:] = _aug_cols(ft, own_first=True)
    m_sc[...] = jnp.full_like(m_sc, -jnp.inf)
    acc_sc[...] = jnp.zeros_like(acc_sc)

    def scores(j, s_ref):
        start = pl.multiple_of(j * bk, bk)
        s_ref[...] = lax.dot_general(q2_sc[...], k2_sc[pl.ds(start, bk), :],
                                     (((1,), (1,)), ((), ())),
                                     preferred_element_type=F32)

    def update(j, s_ref, masked):
        start = pl.multiple_of(j * bk, bk)
        s = s_ref[...]
        if masked:
            diff = (lax.broadcasted_iota(jnp.int32, (bq, bk), 1)
                    - lax.broadcasted_iota(jnp.int32, (bq, bk), 0))
            s = jnp.where(diff <= qi * bq - start, s, -jnp.inf)
        m_prev = m_sc[...]
        m_new = jnp.maximum(m_prev, jnp.max(s, axis=1, keepdims=True))
        p = jnp.exp2(s - jnp.tile(m_new, (1, bk // d)))
        alpha = jnp.exp2(m_prev - m_new)
        pv = jnp.dot(p.astype(BF16), v2_sc[pl.ds(start, bk), :],
                     preferred_element_type=F32)
        acc_sc[...] = jnp.tile(alpha, (1, 2)) * acc_sc[...] + pv
        m_sc[...] = m_new

    scores(0, sa_sc)

    def body(jj, carry):
        j = 2 * jj
        scores(j + 1, sb_sc)
        update(j, sa_sc, False)
        scores(j + 2, sa_sc)
        update(j + 1, sb_sc, False)
        return carry

    lax.fori_loop(0, qi // 2, body, 0)

    @pl.when(qi % 2 == 0)
    def _():
        update(qi, sa_sc, True)

    @pl.when(qi % 2 == 1)
    def _():
        scores(qi, sb_sc)
        update(qi - 1, sa_sc, False)
        update(qi, sb_sc, True)

    o_ref[...] = (acc_sc[:, :d] / acc_sc[:, d:]).astype(o_ref.dtype)


def _attention(qkv3, f3, n_heads, bq=512, chunk=512):
    bsz, seq, _ = qkv3.shape
    bk = bq
    assert seq % bq == 0 and seq % chunk == 0
    kern = functools.partial(_attn_kernel, bq=bq, bk=bk, seq=seq, chunk=chunk)
    d = HEAD_DIM
    return pl.pallas_call(
        kern,
        out_shape=jax.ShapeDtypeStruct((bsz, seq, n_heads * d), F32),
        grid=(bsz, n_heads, seq // bq),
        in_specs=[
            pl.BlockSpec((None, bq, d), lambda b, h, i: (b, i, h)),
            pl.BlockSpec((None, seq, d), lambda b, h, i: (b, 0, n_heads + h)),
            pl.BlockSpec((None, seq, d), lambda b, h, i: (b, 0, 2 * n_heads + h)),
            pl.BlockSpec((None, bq, f3.shape[2]), lambda b, h, i: (b, i, 0)),
            pl.BlockSpec((None, seq, f3.shape[2]), lambda b, h, i: (b, 0, 0)),
        ],
        out_specs=pl.BlockSpec((None, bq, d), lambda b, h, i: (b, i, h)),
        scratch_shapes=[pltpu.VMEM((bq, d + AUG), BF16),
                        pltpu.VMEM((seq, d + AUG), BF16),
                        pltpu.VMEM((seq, 2 * d), BF16),
                        pltpu.VMEM((bq, bk), F32),
                        pltpu.VMEM((bq, bk), F32),
                        pltpu.VMEM((bq, d), F32),
                        pltpu.VMEM((bq, 2 * d), F32)],
        compiler_params=_cparams(("arbitrary", "arbitrary", "arbitrary")),
        name="fox_attention",
    )(qkv3, qkv3, qkv3, f3, f3)


def _mixpost_kernel(ya_ref, gb_ref, gc_ref, u_ref, gch_ref, uh_ref, cw_ref,
                    ga_ref, gcn_ref, o_ref, *, bt, seq, width):
    i = pl.program_id(0)
    z = gc_ref[...] * u_ref[...]
    zp = gch_ref[...] * uh_ref[...]
    zp = jnp.where((i * bt) % seq == 0, 0.0, zp)
    row = lax.broadcasted_iota(jnp.int32, (bt, 1), 0)
    z1 = jnp.where(row == 0, zp[SUBLANES - 1:SUBLANES, :], pltpu.roll(z, 1, 0))
    z2 = jnp.where(row == 0, zp[SUBLANES - 2:SUBLANES - 1, :],
                   jnp.where(row == 1, zp[SUBLANES - 1:SUBLANES, :], pltpu.roll(z, 2, 0)))
    conv = cw_ref[2:3, :] * z + cw_ref[1:2, :] * z1 + cw_ref[0:1, :] * z2
    y_conv = gb_ref[...] * conv
    o_ref[:, :width] = _rms(ya_ref[...], ga_ref[...]).astype(o_ref.dtype)
    o_ref[:, width:] = _rms(y_conv, gcn_ref[...]).astype(o_ref.dtype)


def _mixpost(y_attn, gcu, conv_w, g_attn, g_conv, seq, bt=256):
    t, width = y_attn.shape
    hb = bt // SUBLANES
    kern = functools.partial(_mixpost_kernel, bt=bt, seq=seq, width=width)
    halo = lambda c: pl.BlockSpec((SUBLANES, width),
                                  lambda i: (jnp.maximum(i * hb - 1, 0), c))
    return pl.pallas_call(
        kern,
        out_shape=jax.ShapeDtypeStruct((t, 2 * width), BF16),
        grid=(t // bt,),
        in_specs=[pl.BlockSpec((bt, width), lambda i: (i, 0)),
                  pl.BlockSpec((bt, width), lambda i: (i, 0)),
                  pl.BlockSpec((bt, width), lambda i: (i, 1)),
                  pl.BlockSpec((bt, width), lambda i: (i, 2)),
                  halo(1), halo(2),
                  pl.BlockSpec((CONV_K, width), lambda i: (0, 0)),
                  pl.BlockSpec((1, width), lambda i: (0, 0)),
                  pl.BlockSpec((1, width), lambda i: (0, 0))],
        out_specs=pl.BlockSpec((bt, 2 * width), lambda i: (i, 0)),
        compiler_params=_cparams(("arbitrary",)),
        name="conv_gate_norm",
    )(y_attn, gcu, gcu, gcu, gcu, gcu, conv_w,
      g_attn.reshape(1, width), g_conv.reshape(1, width))


def _res1_kernel(x_ref, h_ref, mod_ref, gpost_ref, gpre_ref, x1_ref, h2_ref):
    x1 = x_ref[...] + mod_ref[2:3, :] * _rms(h_ref[...], gpost_ref[...])
    x1_ref[...] = x1
    y = _rms(x1, gpre_ref[...])
    h2_ref[...] = (y * (1.0 + mod_ref[4:5, :]) + mod_ref[3:4, :]).astype(h2_ref.dtype)


def _res1(x2, h, mod3, g_post, g_pre, seq, bt=256):
    t, d = x2.shape
    row = pl.BlockSpec((bt, d), lambda i: (i, 0))
    vec = pl.BlockSpec((1, d), lambda i: (0, 0))
    return pl.pallas_call(
        _res1_kernel,
        out_shape=(jax.ShapeDtypeStruct((t, d), F32), jax.ShapeDtypeStruct((t, d), BF16)),
        grid=(t // bt,),
        in_specs=[row, row,
                  pl.BlockSpec((None, N_MOD, d), lambda i: ((i * bt) // seq, 0, 0)),
                  vec, vec],
        out_specs=(row, row),
        compiler_params=_cparams(("arbitrary",)),
        name="residual_norm_mod",
    )(x2, h, mod3, g_post.reshape(1, d), g_pre.reshape(1, d))


def _res2_kernel(x_ref, h_ref, mod_ref, gpost_ref, o_ref):
    o_ref[...] = x_ref[...] + mod_ref[5:6, :] * _rms(h_ref[...], gpost_ref[...])


def _res2(x1, h, mod3, g_post, seq, bt=256):
    t, d = x1.shape
    row = pl.BlockSpec((bt, d), lambda i: (i, 0))
    return pl.pallas_call(
        _res2_kernel,
        out_shape=jax.ShapeDtypeStruct((t, d), F32),
        grid=(t // bt,),
        in_specs=[row, row,
                  pl.BlockSpec((None, N_MOD, d), lambda i: ((i * bt) // seq, 0, 0)),
                  pl.BlockSpec((1, d), lambda i: (0, 0))],
        out_specs=row,
        compiler_params=_cparams(("arbitrary",)),
        name="residual_norm",
    )(x1, h, mod3, g_post.reshape(1, d))


def _ffn_up_kernel(x_ref, wg_ref, wu_ref, o_ref):
    x = x_ref[...]
    g = jnp.dot(x, wg_ref[...], preferred_element_type=F32)
    u = jnp.dot(x, wu_ref[...], preferred_element_type=F32)
    o_ref[...] = (g * jax.nn.sigmoid(g) * u).astype(o_ref.dtype)


def _ffn_up(x, wg, wu, bm=1024, bn=256):
    m, k = x.shape
    n = wg.shape[1]
    assert m % bm == 0 and n % bn == 0
    wspec = pl.BlockSpec((k, bn), lambda i, j: (0, j))
    return pl.pallas_call(
        _ffn_up_kernel,
        out_shape=jax.ShapeDtypeStruct((m, n), BF16),
        grid=(m // bm, n // bn),
        in_specs=[pl.BlockSpec((bm, k), lambda i, j: (i, 0)), wspec, wspec],
        out_specs=pl.BlockSpec((bm, bn), lambda i, j: (i, j)),
        compiler_params=_cparams(("arbitrary", "arbitrary")),
        name="swiglu_up",
    )(x, wg, wu)


def kernel(x, c, w_ada, b_ada, pre_norm_mix, w_in, b_f, conv_w, attn_out_norm,
           conv_out_norm, w_out, post_norm_mix, pre_norm_ffn, w_gate, w_up, w_down,
           post_norm_ffn):
    bsz, seq, d = x.shape
    depth = w_ada.shape[0]
    n_heads = b_f.shape[1]
    attn_w = n_heads * HEAD_DIM
    conv_wd = conv_w.shape[2]
    t = bsz * seq
    lane = 128

    x2 = x.reshape(t, d)
    for l in range(depth):
        w_qkv = w_in[l][:, :3 * attn_w].astype(BF16)
        w_f = jnp.pad(w_in[l][:, 3 * attn_w:3 * attn_w + n_heads],
                      ((0, 0), (0, lane - n_heads))).astype(BF16)
        w_gcu = w_in[l][:, 3 * attn_w + n_heads:].astype(BF16)
        w_o = w_out[l].astype(BF16)
        w_g = w_gate[l].astype(BF16)
        w_u = w_up[l].astype(BF16)
        w_d = w_down[l].astype(BF16)
        b_pad = jnp.pad(b_f[l], (0, lane - n_heads)).reshape(1, lane)

        mod3 = _ada(c, w_ada[l], b_ada[l]).reshape(bsz, N_MOD, d)

        h1 = _prenorm(x2, mod3, pre_norm_mix[l], seq)
        qkv = _mm(h1, w_qkv, BF16, 1024, 1024, "in_proj_qkv",
                  scaled_cols=attn_w, scale=HEAD_DIM ** -0.5 * LOG2E)
        gcu = _mm(h1, w_gcu, F32, 1024, 1024, "in_proj_conv")
        f_logit = _mm(h1, w_f, F32, 1024, lane, "in_proj_forget")
        f_cum = _fcum(f_logit.reshape(bsz, seq, lane), b_pad)
        y_attn = _attention(qkv.reshape(bsz, seq, 3 * attn_w), f_cum, n_heads)
        y_cat = _mixpost(y_attn.reshape(t, attn_w), gcu, conv_w[l],
                         attn_out_norm[l], conv_out_norm[l], seq)
        h = _mm(y_cat, w_o, F32, 1024, 1024, "out_proj")
        x2, h2 = _res1(x2, h, mod3, post_norm_mix[l], pre_norm_ffn[l], seq)

        a = _ffn_up(h2, w_g, w_u)
        h3 = _mm(a, w_d, F32, 512, 256, "swiglu_down")
        x2 = _res2(x2, h3, mod3, post_norm_ffn[l], seq)
    return x2.reshape(bsz, seq, d)
```

```python
import functools

import jax
import jax.numpy as jnp
from jax import lax
from jax.experimental import pallas as pl
from jax.experimental.pallas import tpu as pltpu

HEAD_DIM = 128
CONV_K = 3
EPS = 1e-6
N_MOD = 6
V7X_VMEM_BYTES = 64 * 1024 * 1024
VMEM_LIMIT = 56 * 1024 * 1024
SUBLANES = 8

F32 = jnp.float32
BF16 = jnp.bfloat16


def _cparams(sem):
    return pltpu.CompilerParams(dimension_semantics=sem, vmem_limit_bytes=VMEM_LIMIT)


def _rms(x, g):
    return x * lax.rsqrt(jnp.mean(x * x, axis=-1, keepdims=True) + EPS) * g


def _ada_kernel(c_ref, w_ref, b_ref, o_ref):
    c = c_ref[...]
    c_act = (c * jax.nn.sigmoid(c)).astype(BF16)
    o_ref[...] = jnp.dot(c_act, w_ref[...].astype(BF16),
                         preferred_element_type=F32) + b_ref[...]


def _ada(c, w, b, bn=512):
    bsz, d = c.shape
    n = w.shape[1]
    return pl.pallas_call(
        _ada_kernel,
        out_shape=jax.ShapeDtypeStruct((bsz, n), F32),
        grid=(n // bn,),
        in_specs=[pl.BlockSpec((bsz, d), lambda j: (0, 0)),
                  pl.BlockSpec((d, bn), lambda j: (0, j)),
                  pl.BlockSpec((1, bn), lambda j: (0, j))],
        out_specs=pl.BlockSpec((bsz, bn), lambda j: (0, j)),
        compiler_params=_cparams(("arbitrary",)),
        name="ada_mod",
    )(c, w, b.reshape(1, n))


def _prenorm_kernel(x_ref, mod_ref, g_ref, o_ref):
    y = _rms(x_ref[...], g_ref[...])
    o_ref[...] = (y * (1.0 + mod_ref[1:2, :]) + mod_ref[0:1, :]).astype(o_ref.dtype)


def _prenorm(x2, mod3, g, seq, bt=512):
    t, d = x2.shape
    return pl.pallas_call(
        _prenorm_kernel,
        out_shape=jax.ShapeDtypeStruct((t, d), BF16),
        grid=(t // bt,),
        in_specs=[pl.BlockSpec((bt, d), lambda i: (i, 0)),
                  pl.BlockSpec((None, N_MOD, d), lambda i: ((i * bt) // seq, 0, 0)),
                  pl.BlockSpec((1, d), lambda i: (0, 0))],
        out_specs=pl.BlockSpec((bt, d), lambda i: (i, 0)),
        compiler_params=_cparams(("arbitrary",)),
        name="prenorm_mod",
    )(x2, mod3, g.reshape(1, d))


def _mm_kernel(x_ref, w_ref, o_ref, *, scaled_blocks, scale):
    acc = jnp.dot(x_ref[...], w_ref[...].astype(BF16), preferred_element_type=F32)
    if scaled_blocks:
        acc = acc * jnp.where(pl.program_id(1) < scaled_blocks, scale, 1.0)
    o_ref[...] = acc.astype(o_ref.dtype)


def _mm(x, w3, layer, n, out_dtype, bm, bn, name, scaled_cols=0, scale=1.0):
    m, k = x.shape
    assert m % bm == 0 and n % bn == 0 and scaled_cols % bn == 0
    return pl.pallas_call(
        functools.partial(_mm_kernel, scaled_blocks=scaled_cols // bn, scale=scale),
        out_shape=jax.ShapeDtypeStruct((m, n), out_dtype),
        grid=(m // bm, n // bn),
        in_specs=[pl.BlockSpec((bm, k), lambda i, j: (i, 0)),
                  pl.BlockSpec((None, k, bn), lambda i, j: (layer, 0, j))],
        out_specs=pl.BlockSpec((bm, bn), lambda i, j: (i, j)),
        compiler_params=_cparams(("arbitrary", "arbitrary")),
        name=name,
    )(x, w3)


def _fcum_kernel(f_ref, b_ref, o_ref, carry_ref, *, bs):
    @pl.when(pl.program_id(1) == 0)
    def _():
        carry_ref[...] = jnp.zeros_like(carry_ref)

    x = f_ref[...] + b_ref[...]
    log_f = jnp.minimum(x, 0.0) - jnp.log1p(jnp.exp(-jnp.abs(x)))
    row = lax.broadcasted_iota(jnp.int32, (bs, bs), 0)
    col = lax.broadcasted_iota(jnp.int32, (bs, bs), 1)
    tril = (col <= row).astype(F32)
    cum = jnp.dot(tril, log_f, preferred_element_type=F32,
                  precision=lax.Precision.HIGHEST) + carry_ref[...]
    o_ref[...] = cum
    carry_ref[...] = cum[bs - 1:bs, :]


def _fcum(f3, b_pad, bs=512):
    bsz, seq, n = f3.shape
    return pl.pallas_call(
        functools.partial(_fcum_kernel, bs=bs),
        out_shape=jax.ShapeDtypeStruct((bsz, seq, n), F32),
        grid=(bsz, seq // bs),
        in_specs=[pl.BlockSpec((None, bs, n), lambda b, i: (b, i, 0)),
                  pl.BlockSpec((1, n), lambda b, i: (0, 0))],
        out_specs=pl.BlockSpec((None, bs, n), lambda b, i: (b, i, 0)),
        scratch_shapes=[pltpu.VMEM((1, n), F32)],
        compiler_params=_cparams(("arbitrary", "arbitrary")),
        name="forget_cumsum",
    )(f3, b_pad)


LOG2E = 1.4426950408889634
AUG = 16
N_SPLIT = 3


def _aug_cols(f_col, own_first):
    n = f_col.shape[0]
    hi = f_col.astype(BF16).astype(F32)
    r = f_col - hi
    mid = r.astype(BF16).astype(F32)
    lo = (r - mid).astype(BF16).astype(F32)
    lane = lax.broadcasted_iota(jnp.int32, (n, AUG), 1)
    base = 0 if own_first else N_SPLIT
    other = N_SPLIT - base
    ones = jnp.where((lane >= other) & (lane < other + N_SPLIT), 1.0, 0.0)
    vals = jnp.where(lane == base, hi,
                     jnp.where(lane == base + 1, mid,
                               jnp.where(lane == base + 2, lo, ones)))
    return vals.astype(BF16)


def _head_col(f, h):
    lane = lax.broadcasted_iota(jnp.int32, f.shape, 1)
    return jnp.sum(jnp.where(lane == h, f, 0.0), axis=1, keepdims=True)


def _attn_kernel(q_ref, k_ref, v_ref, fq_ref, fs_ref, o_ref,
                 q2_sc, k2_sc, v2_sc, sa_sc, sb_sc, m_sc, acc_sc, *, bq, bk, seq, chunk):
    h = pl.program_id(1)
    qi = pl.program_id(2)
    d = HEAD_DIM

    @pl.when(qi == 0)
    def _():
        def body(c, carry):
            r = pl.multiple_of(c * chunk, chunk)
            fcol = _head_col(fs_ref[pl.ds(r, chunk), :], h) * (-LOG2E)
            k2_sc[pl.ds(r, chunk), :d] = k_ref[pl.ds(r, chunk), :]
            k2_sc[pl.ds(r, chunk), d:] = _aug_cols(fcol, own_first=False)
            v2_sc[pl.ds(r, chunk), :d] = v_ref[pl.ds(r, chunk), :]
            v2_sc[pl.ds(r, chunk), d:] = jnp.ones((chunk, d), BF16)
            return carry
        lax.fori_loop(0, seq // chunk, body, 0)

    ft = _head_col(fq_ref[...], h) * LOG2E
    q2_sc[:, :d] = q_ref[...]
    q2_sc[:, d:] = _aug_cols(ft, own_first=True)
    m_sc[...] = jnp.full_like(m_sc, -jnp.inf)
    acc_sc[...] = jnp.zeros_like(acc_sc)

    def scores(j, s_ref):
        start = pl.multiple_of(j * bk, bk)
        s_ref[...] = lax.dot_general(q2_sc[...], k2_sc[pl.ds(start, bk), :],
                                     (((1,), (1,)), ((), ())),
                                     preferred_element_type=F32)

    def update(j, s_ref, masked):
        start = pl.multiple_of(j * bk, bk)
        s = s_ref[...]
        if masked:
            diff = (lax.broadcasted_iota(jnp.int32, (bq, bk), 1)
                    - lax.broadcasted_iota(jnp.int32, (bq, bk), 0))
            s = jnp.where(diff <= qi * bq - start, s, -jnp.inf)
        m_prev = m_sc[...]
        m_new = jnp.maximum(m_prev, jnp.max(s, axis=1, keepdims=True))
        p = jnp.exp2(s - jnp.tile(m_new, (1, bk // d)))
        alpha = jnp.exp2(m_prev - m_new)
        pv = jnp.dot(p.astype(BF16), v2_sc[pl.ds(start, bk), :],
                     preferred_element_type=F32)
        acc_sc[...] = jnp.tile(alpha, (1, 2)) * acc_sc[...] + pv
        m_sc[...] = m_new

    scores(0, sa_sc)

    def body(jj, carry):
        j = 2 * jj
        scores(j + 1, sb_sc)
        update(j, sa_sc, False)
        scores(j + 2, sa_sc)
        update(j + 1, sb_sc, False)
        return carry

    lax.fori_loop(0, qi // 2, body, 0)

    @pl.when(qi % 2 == 0)
    def _():
        update(qi, sa_sc, True)

    @pl.when(qi % 2 == 1)
    def _():
        scores(qi, sb_sc)
        update(qi - 1, sa_sc, False)
        update(qi, sb_sc, True)

    o_ref[...] = (acc_sc[:, :d] / acc_sc[:, d:]).astype(o_ref.dtype)


def _attention(qkv3, f3, n_heads, bq=1024, chunk=512):
    bsz, seq, _ = qkv3.shape
    bk = bq
    assert seq % bq == 0 and seq % chunk == 0
    kern = functools.partial(_attn_kernel, bq=bq, bk=bk, seq=seq, chunk=chunk)
    d = HEAD_DIM
    return pl.pallas_call(
        kern,
        out_shape=jax.ShapeDtypeStruct((bsz, seq, n_heads * d), F32),
        grid=(bsz, n_heads, seq // bq),
        in_specs=[
            pl.BlockSpec((None, bq, d), lambda b, h, i: (b, i, h)),
            pl.BlockSpec((None, seq, d), lambda b, h, i: (b, 0, n_heads + h)),
            pl.BlockSpec((None, seq, d), lambda b, h, i: (b, 0, 2 * n_heads + h)),
            pl.BlockSpec((None, bq, f3.shape[2]), lambda b, h, i: (b, i, 0)),
            pl.BlockSpec((None, seq, f3.shape[2]), lambda b, h, i: (b, 0, 0)),
        ],
        out_specs=pl.BlockSpec((None, bq, d), lambda b, h, i: (b, i, h)),
        scratch_shapes=[pltpu.VMEM((bq, d + AUG), BF16),
                        pltpu.VMEM((seq, d + AUG), BF16),
                        pltpu.VMEM((seq, 2 * d), BF16),
                        pltpu.VMEM((bq, bk), F32),
                        pltpu.VMEM((bq, bk), F32),
                        pltpu.VMEM((bq, d), F32),
                        pltpu.VMEM((bq, 2 * d), F32)],
        compiler_params=_cparams(("arbitrary", "arbitrary", "arbitrary")),
        name="fox_attention",
    )(qkv3, qkv3, qkv3, f3, f3)


def _mixpost_kernel(ya_ref, gb_ref, gc_ref, u_ref, gch_ref, uh_ref, cw_ref,
                    ga_ref, gcn_ref, o_ref, *, bt, seq, width):
    i = pl.program_id(0)
    z = gc_ref[...] * u_ref[...]
    zp = gch_ref[...] * uh_ref[...]
    zp = jnp.where((i * bt) % seq == 0, 0.0, zp)
    row = lax.broadcasted_iota(jnp.int32, (bt, 1), 0)
    z1 = jnp.where(row == 0, zp[SUBLANES - 1:SUBLANES, :], pltpu.roll(z, 1, 0))
    z2 = jnp.where(row == 0, zp[SUBLANES - 2:SUBLANES - 1, :],
                   jnp.where(row == 1, zp[SUBLANES - 1:SUBLANES, :], pltpu.roll(z, 2, 0)))
    conv = cw_ref[2:3, :] * z + cw_ref[1:2, :] * z1 + cw_ref[0:1, :] * z2
    y_conv = gb_ref[...] * conv
    o_ref[:, :width] = _rms(ya_ref[...], ga_ref[...]).astype(o_ref.dtype)
    o_ref[:, width:] = _rms(y_conv, gcn_ref[...]).astype(o_ref.dtype)


def _mixpost(y_attn, gcu, conv_w, g_attn, g_conv, seq, bt=256):
    t, width = y_attn.shape
    hb = bt // SUBLANES
    kern = functools.partial(_mixpost_kernel, bt=bt, seq=seq, width=width)
    halo = lambda c: pl.BlockSpec((SUBLANES, width),
                                  lambda i: (jnp.maximum(i * hb - 1, 0), c))
    return pl.pallas_call(
        kern,
        out_shape=jax.ShapeDtypeStruct((t, 2 * width), BF16),
        grid=(t // bt,),
        in_specs=[pl.BlockSpec((bt, width), lambda i: (i, 0)),
                  pl.BlockSpec((bt, width), lambda i: (i, 0)),
                  pl.BlockSpec((bt, width), lambda i: (i, 1)),
                  pl.BlockSpec((bt, width), lambda i: (i, 2)),
                  halo(1), halo(2),
                  pl.BlockSpec((CONV_K, width), lambda i: (0, 0)),
                  pl.BlockSpec((1, width), lambda i: (0, 0)),
                  pl.BlockSpec((1, width), lambda i: (0, 0))],
        out_specs=pl.BlockSpec((bt, 2 * width), lambda i: (i, 0)),
        compiler_params=_cparams(("arbitrary",)),
        name="conv_gate_norm",
    )(y_attn, gcu, gcu, gcu, gcu, gcu, conv_w,
      g_attn.reshape(1, width), g_conv.reshape(1, width))


def _res1_kernel(x_ref, h_ref, mod_ref, gpost_ref, gpre_ref, x1_ref, h2_ref):
    x1 = x_ref[...] + mod_ref[2:3, :] * _rms(h_ref[...], gpost_ref[...])
    x1_ref[...] = x1
    y = _rms(x1, gpre_ref[...])
    h2_ref[...] = (y * (1.0 + mod_ref[4:5, :]) + mod_ref[3:4, :]).astype(h2_ref.dtype)


def _res1(x2, h, mod3, g_post, g_pre, seq, bt=256):
    t, d = x2.shape
    row = pl.BlockSpec((bt, d), lambda i: (i, 0))
    vec = pl.BlockSpec((1, d), lambda i: (0, 0))
    return pl.pallas_call(
        _res1_kernel,
        out_shape=(jax.ShapeDtypeStruct((t, d), F32), jax.ShapeDtypeStruct((t, d), BF16)),
        grid=(t // bt,),
        in_specs=[row, row,
                  pl.BlockSpec((None, N_MOD, d), lambda i: ((i * bt) // seq, 0, 0)),
                  vec, vec],
        out_specs=(row, row),
        compiler_params=_cparams(("arbitrary",)),
        name="residual_norm_mod",
    )(x2, h, mod3, g_post.reshape(1, d), g_pre.reshape(1, d))


def _res2_kernel(x_ref, h_ref, mod_ref, gpost_ref, o_ref):
    o_ref[...] = x_ref[...] + mod_ref[5:6, :] * _rms(h_ref[...], gpost_ref[...])


def _res2(x1, h, mod3, g_post, seq, bt=256):
    t, d = x1.shape
    row = pl.BlockSpec((bt, d), lambda i: (i, 0))
    return pl.pallas_call(
        _res2_kernel,
        out_shape=jax.ShapeDtypeStruct((t, d), F32),
        grid=(t // bt,),
        in_specs=[row, row,
                  pl.BlockSpec((None, N_MOD, d), lambda i: ((i * bt) // seq, 0, 0)),
                  pl.BlockSpec((1, d), lambda i: (0, 0))],
        out_specs=row,
        compiler_params=_cparams(("arbitrary",)),
        name="residual_norm",
    )(x1, h, mod3, g_post.reshape(1, d))


def _ffn_up_kernel(x_ref, wg_ref, wu_ref, o_ref):
    x = x_ref[...]
    g = jnp.dot(x, wg_ref[...].astype(BF16), preferred_element_type=F32)
    u = jnp.dot(x, wu_ref[...].astype(BF16), preferred_element_type=F32)
    o_ref[...] = (g * jax.nn.sigmoid(g) * u).astype(o_ref.dtype)


def _ffn_up(x, wg, wu, layer, bm=1024, bn=256):
    m, k = x.shape
    n = wg.shape[2]
    assert m % bm == 0 and n % bn == 0
    wspec = pl.BlockSpec((None, k, bn), lambda i, j: (layer, 0, j))
    return pl.pallas_call(
        _ffn_up_kernel,
        out_shape=jax.ShapeDtypeStruct((m, n), BF16),
        grid=(m // bm, n // bn),
        in_specs=[pl.BlockSpec((bm, k), lambda i, j: (i, 0)), wspec, wspec],
        out_specs=pl.BlockSpec((bm, bn), lambda i, j: (i, j)),
        compiler_params=_cparams(("arbitrary", "arbitrary")),
        name="swiglu_up",
    )(x, wg, wu)


def kernel(x, c, w_ada, b_ada, pre_norm_mix, w_in, b_f, conv_w, attn_out_norm,
           conv_out_norm, w_out, post_norm_mix, pre_norm_ffn, w_gate, w_up, w_down,
           post_norm_ffn):
    bsz, seq, d = x.shape
    depth = w_ada.shape[0]
    n_heads = b_f.shape[1]
    attn_w = n_heads * HEAD_DIM
    conv_wd = conv_w.shape[2]
    t = bsz * seq
    lane = 128

    x2 = x.reshape(t, d)
    for l in range(depth):
        w_f = jnp.pad(w_in[l][:, 3 * attn_w:3 * attn_w + n_heads],
                      ((0, 0), (0, lane - n_heads))).astype(BF16)[None]
        w_gcu = w_in[l][:, 3 * attn_w + n_heads:].astype(BF16)[None]
        w_d = w_down[l].astype(BF16)[None]
        b_pad = jnp.pad(b_f[l], (0, lane - n_heads)).reshape(1, lane)

        mod3 = _ada(c, w_ada[l], b_ada[l]).reshape(bsz, N_MOD, d)

        h1 = _prenorm(x2, mod3, pre_norm_mix[l], seq)
        qkv = _mm(h1, w_in, l, 3 * attn_w, BF16, 1024, 512, "in_proj_qkv",
                  scaled_cols=attn_w, scale=HEAD_DIM ** -0.5 * LOG2E)
        gcu = _mm(h1, w_gcu, 0, 3 * conv_wd, F32, 1024, 1024, "in_proj_conv")
        f_logit = _mm(h1, w_f, 0, lane, F32, 1024, lane, "in_proj_forget")
        f_cum = _fcum(f_logit.reshape(bsz, seq, lane), b_pad)
        y_attn = _attention(qkv.reshape(bsz, seq, 3 * attn_w), f_cum, n_heads)
        y_cat = _mixpost(y_attn.reshape(t, attn_w), gcu, conv_w[l],
                         attn_out_norm[l], conv_out_norm[l], seq)
        h = _mm(y_cat, w_out, l, d, F32, 1024, 512, "out_proj")
        x2, h2 = _res1(x2, h, mod3, post_norm_mix[l], pre_norm_ffn[l], seq)

        a = _ffn_up(h2, w_gate, w_up, l)
        h3 = _mm(a, w_d, 0, d, F32, 512, 256, "swiglu_down")
        x2 = _res2(x2, h3, mod3, post_norm_ffn[l], seq)
    return x2.reshape(bsz, seq, d)
```

```python
import functools

import jax
import jax.numpy as jnp
from jax import lax
from jax.experimental import pallas as pl
from jax.experimental.pallas import tpu as pltpu

HEAD_DIM = 128
CONV_K = 3
EPS = 1e-6
N_MOD = 6
V7X_VMEM_BYTES = 64 * 1024 * 1024
VMEM_LIMIT = 56 * 1024 * 1024
SUBLANES = 8

F32 = jnp.float32
BF16 = jnp.bfloat16


def _cparams(sem):
    return pltpu.CompilerParams(dimension_semantics=sem, vmem_limit_bytes=VMEM_LIMIT)


def _rms(x, g):
    return x * lax.rsqrt(jnp.mean(x * x, axis=-1, keepdims=True) + EPS) * g


def _ada_kernel(c_ref, w_ref, b_ref, o_ref):
    c = c_ref[...]
    c_act = (c * jax.nn.sigmoid(c)).astype(BF16)
    o_ref[...] = jnp.dot(c_act, w_ref[...].astype(BF16),
                         preferred_element_type=F32) + b_ref[...]


def _ada(c, w, b, bn=512):
    bsz, d = c.shape
    n = w.shape[1]
    return pl.pallas_call(
        _ada_kernel,
        out_shape=jax.ShapeDtypeStruct((bsz, n), F32),
        grid=(n // bn,),
        in_specs=[pl.BlockSpec((bsz, d), lambda j: (0, 0)),
                  pl.BlockSpec((d, bn), lambda j: (0, j)),
                  pl.BlockSpec((1, bn), lambda j: (0, j))],
        out_specs=pl.BlockSpec((bsz, bn), lambda j: (0, j)),
        compiler_params=_cparams(("arbitrary",)),
        name="ada_mod",
    )(c, w, b.reshape(1, n))


def _prenorm_kernel(x_ref, mod_ref, g_ref, o_ref):
    y = _rms(x_ref[...], g_ref[...])
    o_ref[...] = (y * (1.0 + mod_ref[1:2, :]) + mod_ref[0:1, :]).astype(o_ref.dtype)


def _prenorm(x2, mod3, g, seq, bt=512):
    t, d = x2.shape
    return pl.pallas_call(
        _prenorm_kernel,
        out_shape=jax.ShapeDtypeStruct((t, d), BF16),
        grid=(t // bt,),
        in_specs=[pl.BlockSpec((bt, d), lambda i: (i, 0)),
                  pl.BlockSpec((None, N_MOD, d), lambda i: ((i * bt) // seq, 0, 0)),
                  pl.BlockSpec((1, d), lambda i: (0, 0))],
        out_specs=pl.BlockSpec((bt, d), lambda i: (i, 0)),
        compiler_params=_cparams(("arbitrary",)),
        name="prenorm_mod",
    )(x2, mod3, g.reshape(1, d))


def _mm_kernel(x_ref, w_ref, o_ref, *, w_is_nk, scaled_blocks, scale):
    contract_w = 1 if w_is_nk else 0
    acc = lax.dot_general(x_ref[...], w_ref[...], (((1,), (contract_w,)), ((), ())),
                          preferred_element_type=F32)
    if scaled_blocks:
        acc = acc * jnp.where(pl.program_id(1) < scaled_blocks, scale, 1.0)
    o_ref[...] = acc.astype(o_ref.dtype)


def _mm(x, w, n, out_dtype, bm, bn, name, w_is_nk=False, scaled_cols=0, scale=1.0):
    m, k = x.shape
    assert m % bm == 0 and n % bn == 0 and scaled_cols % bn == 0
    if w_is_nk:
        w_spec = pl.BlockSpec((bn, k), lambda i, j: (j, 0))
    else:
        w_spec = pl.BlockSpec((k, bn), lambda i, j: (0, j))
    return pl.pallas_call(
        functools.partial(_mm_kernel, w_is_nk=w_is_nk,
                          scaled_blocks=scaled_cols // bn, scale=scale),
        out_shape=jax.ShapeDtypeStruct((m, n), out_dtype),
        grid=(m // bm, n // bn),
        in_specs=[pl.BlockSpec((bm, k), lambda i, j: (i, 0)), w_spec],
        out_specs=pl.BlockSpec((bm, bn), lambda i, j: (i, j)),
        compiler_params=_cparams(("arbitrary", "arbitrary")),
        name=name,
    )(x, w)


def _fcum_kernel(f_ref, b_ref, o_ref, carry_ref, *, bs):
    @pl.when(pl.program_id(1) == 0)
    def _():
        carry_ref[...] = jnp.zeros_like(carry_ref)

    x = f_ref[...] + b_ref[...]
    log_f = jnp.minimum(x, 0.0) - jnp.log1p(jnp.exp(-jnp.abs(x)))
    row = lax.broadcasted_iota(jnp.int32, (bs, bs), 0)
    col = lax.broadcasted_iota(jnp.int32, (bs, bs), 1)
    tril = (col <= row).astype(F32)
    cum = jnp.dot(tril, log_f, preferred_element_type=F32,
                  precision=lax.Precision.HIGHEST) + carry_ref[...]
    o_ref[...] = cum
    carry_ref[...] = cum[bs - 1:bs, :]


def _fcum(f3, b_pad, bs=512):
    bsz, seq, n = f3.shape
    return pl.pallas_call(
        functools.partial(_fcum_kernel, bs=bs),
        out_shape=jax.ShapeDtypeStruct((bsz, seq, n), F32),
        grid=(bsz, seq // bs),
        in_specs=[pl.BlockSpec((None, bs, n), lambda b, i: (b, i, 0)),
                  pl.BlockSpec((1, n), lambda b, i: (0, 0))],
        out_specs=pl.BlockSpec((None, bs, n), lambda b, i: (b, i, 0)),
        scratch_shapes=[pltpu.VMEM((1, n), F32)],
        compiler_params=_cparams(("arbitrary", "arbitrary")),
        name="forget_cumsum",
    )(f3, b_pad)


LOG2E = 1.4426950408889634
AUG = 16
N_SPLIT = 3


def _aug_cols(f_col, own_first):
    n = f_col.shape[0]
    hi = f_col.astype(BF16).astype(F32)
    r = f_col - hi
    mid = r.astype(BF16).astype(F32)
    lo = (r - mid).astype(BF16).astype(F32)
    lane = lax.broadcasted_iota(jnp.int32, (n, AUG), 1)
    base = 0 if own_first else N_SPLIT
    other = N_SPLIT - base
    ones = jnp.where((lane >= other) & (lane < other + N_SPLIT), 1.0, 0.0)
    vals = jnp.where(lane == base, hi,
                     jnp.where(lane == base + 1, mid,
                               jnp.where(lane == base + 2, lo, ones)))
    return vals.astype(BF16)


def _head_col(f, h):
    lane = lax.broadcasted_iota(jnp.int32, f.shape, 1)
    return jnp.sum(jnp.where(lane == h, f, 0.0), axis=1, keepdims=True)


def _attn_kernel(q_ref, k_ref, v_ref, fq_ref, fs_ref, o_ref,
                 q2_sc, k2_sc, v2_sc, sa_sc, sb_sc, m_sc, acc_sc, *, bq, bk, seq, chunk):
    h = pl.program_id(1)
    qi = pl.program_id(2)
    d = HEAD_DIM

    @pl.when(qi == 0)
    def _():
        def body(c, carry):
            r = pl.multiple_of(c * chunk, chunk)
            fcol = _head_col(fs_ref[pl.ds(r, chunk), :], h) * (-LOG2E)
            k2_sc[pl.ds(r, chunk), :d] = k_ref[pl.ds(r, chunk), :]
            k2_sc[pl.ds(r, chunk), d:] = _aug_cols(fcol, own_first=False)
            v2_sc[pl.ds(r, chunk), :d] = v_ref[pl.ds(r, chunk), :]
            v2_sc[pl.ds(r, chunk), d:] = jnp.ones((chunk, d), BF16)
            return carry
        lax.fori_loop(0, seq // chunk, body, 0)

    ft = _head_col(fq_ref[...], h) * LOG2E
    q2_sc[:, :d] = q_ref[...]
    q2_sc[:, d:] = _aug_cols(ft, own_first=True)
    m_sc[...] = jnp.full_like(m_sc, -jnp.inf)
    acc_sc[...] = jnp.zeros_like(acc_sc)

    def scores(j, s_ref):
        start = pl.multiple_of(j * bk, bk)
        s_ref[...] = lax.dot_general(q2_sc[...], k2_sc[pl.ds(start, bk), :],
                                     (((1,), (1,)), ((), ())),
                                     preferred_element_type=F32)

    def update(j, s_ref, masked):
        start = pl.multiple_of(j * bk, bk)
        s = s_ref[...]
        if masked:
            diff = (lax.broadcasted_iota(jnp.int32, (bq, bk), 1)
                    - lax.broadcasted_iota(jnp.int32, (bq, bk), 0))
            s = jnp.where(diff <= qi * bq - start, s, -jnp.inf)
        m_prev = m_sc[...]
        m_new = jnp.maximum(m_prev, jnp.max(s, axis=1, keepdims=True))
        p = jnp.exp2(s - jnp.tile(m_new, (1, bk // d)))
        alpha = jnp.exp2(m_prev - m_new)
        pv = jnp.dot(p.astype(BF16), v2_sc[pl.ds(start, bk), :],
                     preferred_element_type=F32)
        acc_sc[...] = jnp.tile(alpha, (1, 2)) * acc_sc[...] + pv
        m_sc[...] = m_new

    scores(0, sa_sc)

    def body(jj, carry):
        j = 2 * jj
        scores(j + 1, sb_sc)
        update(j, sa_sc, False)
        scores(j + 2, sa_sc)
        update(j + 1, sb_sc, False)
        return carry

    lax.fori_loop(0, qi // 2, body, 0)

    @pl.when(qi % 2 == 0)
    def _():
        update(qi, sa_sc, True)

    @pl.when(qi % 2 == 1)
    def _():
        scores(qi, sb_sc)
        update(qi - 1, sa_sc, False)
        update(qi, sb_sc, True)

    o_ref[...] = (acc_sc[:, :d] / acc_sc[:, d:]).astype(o_ref.dtype)


def _attention(qkv3, f3, n_heads, bq=1024, chunk=512):
    bsz, seq, _ = qkv3.shape
    bk = bq
    assert seq % bq == 0 and seq % chunk == 0
    kern = functools.partial(_attn_kernel, bq=bq, bk=bk, seq=seq, chunk=chunk)
    d = HEAD_DIM
    return pl.pallas_call(
        kern,
        out_shape=jax.ShapeDtypeStruct((bsz, seq, n_heads * d), F32),
        grid=(bsz, n_heads, seq // bq),
        in_specs=[
            pl.BlockSpec((None, bq, d), lambda b, h, i: (b, i, h)),
            pl.BlockSpec((None, seq, d), lambda b, h, i: (b, 0, n_heads + h)),
            pl.BlockSpec((None, seq, d), lambda b, h, i: (b, 0, 2 * n_heads + h)),
            pl.BlockSpec((None, bq, f3.shape[2]), lambda b, h, i: (b, i, 0)),
            pl.BlockSpec((None, seq, f3.shape[2]), lambda b, h, i: (b, 0, 0)),
        ],
        out_specs=pl.BlockSpec((None, bq, d), lambda b, h, i: (b, i, h)),
        scratch_shapes=[pltpu.VMEM((bq, d + AUG), BF16),
                        pltpu.VMEM((seq, d + AUG), BF16),
                        pltpu.VMEM((seq, 2 * d), BF16),
                        pltpu.VMEM((bq, bk), F32),
                        pltpu.VMEM((bq, bk), F32),
                        pltpu.VMEM((bq, d), F32),
                        pltpu.VMEM((bq, 2 * d), F32)],
        compiler_params=_cparams(("arbitrary", "arbitrary", "arbitrary")),
        name="fox_attention",
    )(qkv3, qkv3, qkv3, f3, f3)


def _mixpost_kernel(ya_ref, gb_ref, gc_ref, u_ref, gch_ref, uh_ref, cw_ref,
                    ga_ref, gcn_ref, o_ref, *, bt, seq, width):
    i = pl.program_id(0)
    z = gc_ref[...] * u_ref[...]
    zp = gch_ref[...] * uh_ref[...]
    zp = jnp.where((i * bt) % seq == 0, 0.0, zp)
    row = lax.broadcasted_iota(jnp.int32, (bt, 1), 0)
    z1 = jnp.where(row == 0, zp[SUBLANES - 1:SUBLANES, :], pltpu.roll(z, 1, 0))
    z2 = jnp.where(row == 0, zp[SUBLANES - 2:SUBLANES - 1, :],
                   jnp.where(row == 1, zp[SUBLANES - 1:SUBLANES, :], pltpu.roll(z, 2, 0)))
    conv = cw_ref[2:3, :] * z + cw_ref[1:2, :] * z1 + cw_ref[0:1, :] * z2
    y_conv = gb_ref[...] * conv
    o_ref[:, :width] = _rms(ya_ref[...], ga_ref[...]).astype(o_ref.dtype)
    o_ref[:, width:] = _rms(y_conv, gcn_ref[...]).astype(o_ref.dtype)


def _mixpost(y_attn, gcu, conv_w, g_attn, g_conv, seq, bt=256):
    t, width = y_attn.shape
    hb = bt // SUBLANES
    kern = functools.partial(_mixpost_kernel, bt=bt, seq=seq, width=width)
    halo = lambda c: pl.BlockSpec((SUBLANES, width),
                                  lambda i: (jnp.maximum(i * hb - 1, 0), c))
    return pl.pallas_call(
        kern,
        out_shape=jax.ShapeDtypeStruct((t, 2 * width), BF16),
        grid=(t // bt,),
        in_specs=[pl.BlockSpec((bt, width), lambda i: (i, 0)),
                  pl.BlockSpec((bt, width), lambda i: (i, 0)),
                  pl.BlockSpec((bt, width), lambda i: (i, 1)),
                  pl.BlockSpec((bt, width), lambda i: (i, 2)),
                  halo(1), halo(2),
                  pl.BlockSpec((CONV_K, width), lambda i: (0, 0)),
                  pl.BlockSpec((1, width), lambda i: (0, 0)),
                  pl.BlockSpec((1, width), lambda i: (0, 0))],
        out_specs=pl.BlockSpec((bt, 2 * width), lambda i: (i, 0)),
        compiler_params=_cparams(("arbitrary",)),
        name="conv_gate_norm",
    )(y_attn, gcu, gcu, gcu, gcu, gcu, conv_w,
      g_attn.reshape(1, width), g_conv.reshape(1, width))


def _res1_kernel(x_ref, h_ref, mod_ref, gpost_ref, gpre_ref, x1_ref, h2_ref):
    x1 = x_ref[...] + mod_ref[2:3, :] * _rms(h_ref[...], gpost_ref[...])
    x1_ref[...] = x1
    y = _rms(x1, gpre_ref[...])
    h2_ref[...] = (y * (1.0 + mod_ref[4:5, :]) + mod_ref[3:4, :]).astype(h2_ref.dtype)


def _res1(x2, h, mod3, g_post, g_pre, seq, bt=256):
    t, d = x2.shape
    row = pl.BlockSpec((bt, d), lambda i: (i, 0))
    vec = pl.BlockSpec((1, d), lambda i: (0, 0))
    return pl.pallas_call(
        _res1_kernel,
        out_shape=(jax.ShapeDtypeStruct((t, d), F32), jax.ShapeDtypeStruct((t, d), BF16)),
        grid=(t // bt,),
        in_specs=[row, row,
                  pl.BlockSpec((None, N_MOD, d), lambda i: ((i * bt) // seq, 0, 0)),
                  vec, vec],
        out_specs=(row, row),
        compiler_params=_cparams(("arbitrary",)),
        name="residual_norm_mod",
    )(x2, h, mod3, g_post.reshape(1, d), g_pre.reshape(1, d))


def _res2_kernel(x_ref, h_ref, mod_ref, gpost_ref, o_ref):
    o_ref[...] = x_ref[...] + mod_ref[5:6, :] * _rms(h_ref[...], gpost_ref[...])


def _res2(x1, h, mod3, g_post, seq, bt=256):
    t, d = x1.shape
    row = pl.BlockSpec((bt, d), lambda i: (i, 0))
    return pl.pallas_call(
        _res2_kernel,
        out_shape=jax.ShapeDtypeStruct((t, d), F32),
        grid=(t // bt,),
        in_specs=[row, row,
                  pl.BlockSpec((None, N_MOD, d), lambda i: ((i * bt) // seq, 0, 0)),
                  pl.BlockSpec((1, d), lambda i: (0, 0))],
        out_specs=row,
        compiler_params=_cparams(("arbitrary",)),
        name="residual_norm",
    )(x1, h, mod3, g_post.reshape(1, d))


def _ffn_up_kernel(x_ref, wg_ref, wu_ref, o_ref):
    x = x_ref[...]
    g = jnp.dot(x, wg_ref[...].astype(BF16), preferred_element_type=F32)
    u = jnp.dot(x, wu_ref[...].astype(BF16), preferred_element_type=F32)
    o_ref[...] = (g * jax.nn.sigmoid(g) * u).astype(o_ref.dtype)


def _ffn_up(x, wg, wu, layer, bm=2048, bn=256):
    m, k = x.shape
    n = wg.shape[2]
    assert m % bm == 0 and n % bn == 0
    wspec = pl.BlockSpec((None, k, bn), lambda i, j: (layer, 0, j))
    xspec = pl.BlockSpec((bm, k), lambda i, j: (i, 0), pipeline_mode=pl.Buffered(1))
    return pl.pallas_call(
        _ffn_up_kernel,
        out_shape=jax.ShapeDtypeStruct((m, n), BF16),
        grid=(m // bm, n // bn),
        in_specs=[xspec, wspec, wspec],
        out_specs=pl.BlockSpec((bm, bn), lambda i, j: (i, j)),
        compiler_params=_cparams(("arbitrary", "arbitrary")),
        name="swiglu_up",
    )(x, wg, wu)


def kernel(x, c, w_ada, b_ada, pre_norm_mix, w_in, b_f, conv_w, attn_out_norm,
           conv_out_norm, w_out, post_norm_mix, pre_norm_ffn, w_gate, w_up, w_down,
           post_norm_ffn):
    bsz, seq, d = x.shape
    depth = w_ada.shape[0]
    n_heads = b_f.shape[1]
    attn_w = n_heads * HEAD_DIM
    conv_wd = conv_w.shape[2]
    t = bsz * seq
    lane = 128

    x2 = x.reshape(t, d)
    for l in range(depth):
        w_in_nk = jnp.transpose(w_in[l]).astype(BF16)
        f0 = 3 * attn_w
        w_f_nk = jnp.pad(w_in_nk[f0:f0 + n_heads], ((0, lane - n_heads), (0, 0)))
        w_gcu_nk = w_in_nk[f0 + n_heads:]
        w_o = w_out[l].astype(BF16)
        w_d = w_down[l].astype(BF16)
        b_pad = jnp.pad(b_f[l], (0, lane - n_heads)).reshape(1, lane)

        mod3 = _ada(c, w_ada[l], b_ada[l]).reshape(bsz, N_MOD, d)

        h1 = _prenorm(x2, mod3, pre_norm_mix[l], seq)
        qkv = _mm(h1, w_in_nk, 3 * attn_w, BF16, 1024, 1024, "in_proj_qkv", w_is_nk=True,
                  scaled_cols=attn_w, scale=HEAD_DIM ** -0.5 * LOG2E)
        gcu = _mm(h1, w_gcu_nk, 3 * conv_wd, F32, 1024, 1024, "in_proj_conv", w_is_nk=True)
        f_logit = _mm(h1, w_f_nk, lane, F32, 1024, lane, "in_proj_forget", w_is_nk=True)
        f_cum = _fcum(f_logit.reshape(bsz, seq, lane), b_pad)
        y_attn = _attention(qkv.reshape(bsz, seq, 3 * attn_w), f_cum, n_heads)
        y_cat = _mixpost(y_attn.reshape(t, attn_w), gcu, conv_w[l],
                         attn_out_norm[l], conv_out_norm[l], seq)
        h = _mm(y_cat, w_o, d, F32, 1024, 1024, "out_proj")
        x2, h2 = _res1(x2, h, mod3, post_norm_mix[l], pre_norm_ffn[l], seq)

        a = _ffn_up(h2, w_gate, w_up, l)
        h3 = _mm(a, w_d, d, F32, 512, 512, "swiglu_down")
        x2 = _res2(x2, h3, mod3, post_norm_ffn[l], seq)
    return x2.reshape(bsz, seq, d)
```

```python
import functools

import jax
import jax.numpy as jnp
from jax import lax
from jax.experimental import pallas as pl
from jax.experimental.pallas import tpu as pltpu

HEAD_DIM = 128
CONV_K = 3
EPS = 1e-6
N_MOD = 6
V7X_VMEM_BYTES = 64 * 1024 * 1024
VMEM_LIMIT = V7X_VMEM_BYTES - 4 * 1024 * 1024
SUBLANES = 8

F32 = jnp.float32
BF16 = jnp.bfloat16


def _cparams(sem):
    return pltpu.CompilerParams(dimension_semantics=sem, vmem_limit_bytes=VMEM_LIMIT)


def _rms(x, g):
    return x * lax.rsqrt(jnp.mean(x * x, axis=-1, keepdims=True) + EPS) * g


def _ada_kernel(c_ref, w_ref, b_ref, o_ref):
    c = c_ref[...]
    c_act = (c * jax.nn.sigmoid(c)).astype(BF16)
    o_ref[...] = jnp.dot(c_act, w_ref[...].astype(BF16),
                         preferred_element_type=F32) + b_ref[...]


def _ada(c, w, b, bn=512):
    bsz, d = c.shape
    n = w.shape[1]
    return pl.pallas_call(
        _ada_kernel,
        out_shape=jax.ShapeDtypeStruct((bsz, n), F32),
        grid=(n // bn,),
        in_specs=[pl.BlockSpec((bsz, d), lambda j: (0, 0)),
                  pl.BlockSpec((d, bn), lambda j: (0, j)),
                  pl.BlockSpec((1, bn), lambda j: (0, j))],
        out_specs=pl.BlockSpec((bsz, bn), lambda j: (0, j)),
        compiler_params=_cparams(("arbitrary",)),
        name="ada_mod",
    )(c, w, b.reshape(1, n))


def _prenorm_kernel(x_ref, mod_ref, g_ref, o_ref):
    y = _rms(x_ref[...], g_ref[...])
    o_ref[...] = (y * (1.0 + mod_ref[1:2, :]) + mod_ref[0:1, :]).astype(o_ref.dtype)


def _prenorm(x2, mod3, g, seq, bt=512):
    t, d = x2.shape
    return pl.pallas_call(
        _prenorm_kernel,
        out_shape=jax.ShapeDtypeStruct((t, d), BF16),
        grid=(t // bt,),
        in_specs=[pl.BlockSpec((bt, d), lambda i: (i, 0)),
                  pl.BlockSpec((None, N_MOD, d), lambda i: ((i * bt) // seq, 0, 0)),
                  pl.BlockSpec((1, d), lambda i: (0, 0))],
        out_specs=pl.BlockSpec((bt, d), lambda i: (i, 0)),
        compiler_params=_cparams(("arbitrary",)),
        name="prenorm_mod",
    )(x2, mod3, g.reshape(1, d))


def _mm_kernel(x_ref, w_ref, o_ref, *, w_is_nk, scaled_blocks, scale):
    contract_w = 1 if w_is_nk else 0
    acc = lax.dot_general(x_ref[...], w_ref[...], (((1,), (contract_w,)), ((), ())),
                          preferred_element_type=F32)
    if scaled_blocks:
        acc = acc * jnp.where(pl.program_id(1) < scaled_blocks, scale, 1.0)
    o_ref[...] = acc.astype(o_ref.dtype)


def _mm(x, w, n, out_dtype, bm, bn, name, w_is_nk=False, scaled_cols=0, scale=1.0):
    m, k = x.shape
    assert m % bm == 0 and n % bn == 0 and scaled_cols % bn == 0
    if w_is_nk:
        w_spec = pl.BlockSpec((bn, k), lambda i, j: (j, 0))
    else:
        w_spec = pl.BlockSpec((k, bn), lambda i, j: (0, j))
    return pl.pallas_call(
        functools.partial(_mm_kernel, w_is_nk=w_is_nk,
                          scaled_blocks=scaled_cols // bn, scale=scale),
        out_shape=jax.ShapeDtypeStruct((m, n), out_dtype),
        grid=(m // bm, n // bn),
        in_specs=[pl.BlockSpec((bm, k), lambda i, j: (i, 0)), w_spec],
        out_specs=pl.BlockSpec((bm, bn), lambda i, j: (i, j)),
        compiler_params=_cparams(("arbitrary", "arbitrary")),
        name=name,
    )(x, w)


def _fcum_kernel(f_ref, b_ref, o_ref, carry_ref, *, bs):
    @pl.when(pl.program_id(1) == 0)
    def _():
        carry_ref[...] = jnp.zeros_like(carry_ref)

    x = f_ref[...] + b_ref[...]
    log_f = jnp.minimum(x, 0.0) - jnp.log1p(jnp.exp(-jnp.abs(x)))
    row = lax.broadcasted_iota(jnp.int32, (bs, bs), 0)
    col = lax.broadcasted_iota(jnp.int32, (bs, bs), 1)
    tril = (col <= row).astype(F32)
    cum = jnp.dot(tril, log_f, preferred_element_type=F32,
                  precision=lax.Precision.HIGHEST) + carry_ref[...]
    o_ref[...] = cum
    carry_ref[...] = cum[bs - 1:bs, :]


def _fcum(f3, b_pad, bs=512):
    bsz, seq, n = f3.shape
    return pl.pallas_call(
        functools.partial(_fcum_kernel, bs=bs),
        out_shape=jax.ShapeDtypeStruct((bsz, seq, n), F32),
        grid=(bsz, seq // bs),
        in_specs=[pl.BlockSpec((None, bs, n), lambda b, i: (b, i, 0)),
                  pl.BlockSpec((1, n), lambda b, i: (0, 0))],
        out_specs=pl.BlockSpec((None, bs, n), lambda b, i: (b, i, 0)),
        scratch_shapes=[pltpu.VMEM((1, n), F32)],
        compiler_params=_cparams(("arbitrary", "arbitrary")),
        name="forget_cumsum",
    )(f3, b_pad)


LOG2E = 1.4426950408889634
AUG = 16
N_SPLIT = 3


def _aug_cols(f_col, own_first):
    n = f_col.shape[0]
    hi = f_col.astype(BF16).astype(F32)
    r = f_col - hi
    mid = r.astype(BF16).astype(F32)
    lo = (r - mid).astype(BF16).astype(F32)
    lane = lax.broadcasted_iota(jnp.int32, (n, AUG), 1)
    base = 0 if own_first else N_SPLIT
    other = N_SPLIT - base
    ones = jnp.where((lane >= other) & (lane < other + N_SPLIT), 1.0, 0.0)
    vals = jnp.where(lane == base, hi,
                     jnp.where(lane == base + 1, mid,
                               jnp.where(lane == base + 2, lo, ones)))
    return vals.astype(BF16)


def _head_col(f, h):
    lane = lax.broadcasted_iota(jnp.int32, f.shape, 1)
    return jnp.sum(jnp.where(lane == h, f, 0.0), axis=1, keepdims=True)


def _attn_kernel(q_ref, k_ref, v_ref, f_ref, o_ref,
                 q2_sc, k2_sc, v2_sc, sa_sc, sb_sc, m_sc, acc_sc, *, blk, seq, chunk):
    h = pl.program_id(1)
    d = HEAD_DIM
    n_tiles = seq // blk
    n_pairs = n_tiles * (n_tiles + 1) // 2
    assert n_pairs % 2 == 0 and n_pairs >= 2

    def build(c, carry):
        r = pl.multiple_of(c * chunk, chunk)
        fcol = _head_col(f_ref[pl.ds(r, chunk), :], h) * LOG2E
        q2_sc[pl.ds(r, chunk), :d] = q_ref[pl.ds(r, chunk), :]
        q2_sc[pl.ds(r, chunk), d:] = _aug_cols(fcol, own_first=True)
        k2_sc[pl.ds(r, chunk), :d] = k_ref[pl.ds(r, chunk), :]
        k2_sc[pl.ds(r, chunk), d:] = _aug_cols(-fcol, own_first=False)
        v2_sc[pl.ds(r, chunk), :d] = v_ref[pl.ds(r, chunk), :]
        v2_sc[pl.ds(r, chunk), d:] = jnp.ones((chunk, d), BF16)
        return carry

    lax.fori_loop(0, seq // chunk, build, 0)
    m_sc[...] = jnp.full_like(m_sc, -jnp.inf)
    acc_sc[...] = jnp.zeros_like(acc_sc)

    def scores(qi, j, s_ref):
        q0 = pl.multiple_of(qi * blk, blk)
        k0 = pl.multiple_of(j * blk, blk)
        s_ref[...] = lax.dot_general(q2_sc[pl.ds(q0, blk), :], k2_sc[pl.ds(k0, blk), :],
                                     (((1,), (1,)), ((), ())),
                                     preferred_element_type=F32)

    def update(qi, j, s_ref, diagonal):
        k0 = pl.multiple_of(j * blk, blk)
        s = s_ref[...]
        if diagonal:
            col = lax.broadcasted_iota(jnp.int32, (blk, blk), 1)
            row = lax.broadcasted_iota(jnp.int32, (blk, blk), 0)
            s = jnp.where(col <= row, s, -jnp.inf)
        m_prev = jnp.where(j == 0, -jnp.inf, m_sc[...])
        m_new = jnp.maximum(m_prev, jnp.max(s, axis=1, keepdims=True))
        p = jnp.exp2(s - jnp.tile(m_new, (1, blk // d)))
        alpha = jnp.exp2(m_prev - m_new)
        pv = jnp.dot(p.astype(BF16), v2_sc[pl.ds(k0, blk), :],
                     preferred_element_type=F32)
        acc = jnp.tile(alpha, (1, 2)) * acc_sc[...] + pv
        acc_sc[...] = acc
        m_sc[...] = m_new
        if diagonal:
            q0 = pl.multiple_of(qi * blk, blk)
            o_ref[pl.ds(q0, blk), :] = (acc[:, :d] / acc[:, d:]).astype(o_ref.dtype)

    def successor(qi, j):
        last = j == qi
        return jnp.where(last, qi + 1, qi), jnp.where(last, 0, j + 1)

    def stage(qi, j, s_cur, s_next, prefetch=True):
        qi1, j1 = successor(qi, j)
        for diagonal in (False, True):
            @pl.when((j == qi) if diagonal else (j != qi))
            def _():
                if prefetch:
                    scores(qi1, j1, s_next)
                update(qi, j, s_cur, diagonal)
        return qi1, j1

    scores(0, 0, sa_sc)

    def body(it, carry):
        qi, j = stage(*carry, sa_sc, sb_sc)
        return stage(qi, j, sb_sc, sa_sc)

    zero = jnp.int32(0)
    qi, j = lax.fori_loop(0, n_pairs // 2 - 1, body, (zero, zero))
    qi, j = stage(qi, j, sa_sc, sb_sc)
    stage(qi, j, sb_sc, sa_sc, prefetch=False)


def _attention(qkv3, f3, n_heads, blk=1024, chunk=512):
    bsz, seq, _ = qkv3.shape
    assert seq % blk == 0 and seq % chunk == 0
    kern = functools.partial(_attn_kernel, blk=blk, seq=seq, chunk=chunk)
    d = HEAD_DIM
    head_block = lambda first: pl.BlockSpec((None, seq, d), lambda b, h: (b, 0, first + h))
    return pl.pallas_call(
        kern,
        out_shape=jax.ShapeDtypeStruct((bsz, seq, n_heads * d), F32),
        grid=(bsz, n_heads),
        in_specs=[
            head_block(0), head_block(n_heads), head_block(2 * n_heads),
            pl.BlockSpec((None, seq, f3.shape[2]), lambda b, h: (b, 0, 0),
                         pipeline_mode=pl.Buffered(1)),
        ],
        out_specs=head_block(0),
        scratch_shapes=[pltpu.VMEM((seq, d + AUG), BF16),
                        pltpu.VMEM((seq, d + AUG), BF16),
                        pltpu.VMEM((seq, 2 * d), BF16),
                        pltpu.VMEM((blk, blk), F32),
                        pltpu.VMEM((blk, blk), F32),
                        pltpu.VMEM((blk, d), F32),
                        pltpu.VMEM((blk, 2 * d), F32)],
        compiler_params=_cparams(("arbitrary", "arbitrary")),
        name="fox_attention",
    )(qkv3, qkv3, qkv3, f3)


def _mixpost_kernel(ya_ref, gb_ref, gc_ref, u_ref, gch_ref, uh_ref, cw_ref,
                    ga_ref, gcn_ref, o_ref, *, bt, seq, width):
    i = pl.program_id(0)
    z = gc_ref[...] * u_ref[...]
    zp = gch_ref[...] * uh_ref[...]
    zp = jnp.where((i * bt) % seq == 0, 0.0, zp)
    row = lax.broadcasted_iota(jnp.int32, (bt, 1), 0)
    z1 = jnp.where(row == 0, zp[SUBLANES - 1:SUBLANES, :], pltpu.roll(z, 1, 0))
    z2 = jnp.where(row == 0, zp[SUBLANES - 2:SUBLANES - 1, :],
                   jnp.where(row == 1, zp[SUBLANES - 1:SUBLANES, :], pltpu.roll(z, 2, 0)))
    conv = cw_ref[2:3, :] * z + cw_ref[1:2, :] * z1 + cw_ref[0:1, :] * z2
    y_conv = gb_ref[...] * conv
    o_ref[:, :width] = _rms(ya_ref[...], ga_ref[...]).astype(o_ref.dtype)
    o_ref[:, width:] = _rms(y_conv, gcn_ref[...]).astype(o_ref.dtype)


def _mixpost(y_attn, gcu, conv_w, g_attn, g_conv, seq, bt=256):
    t, width = y_attn.shape
    hb = bt // SUBLANES
    kern = functools.partial(_mixpost_kernel, bt=bt, seq=seq, width=width)
    halo = lambda c: pl.BlockSpec((SUBLANES, width),
                                  lambda i: (jnp.maximum(i * hb - 1, 0), c))
    return pl.pallas_call(
        kern,
        out_shape=jax.ShapeDtypeStruct((t, 2 * width), BF16),
        grid=(t // bt,),
        in_specs=[pl.BlockSpec((bt, width), lambda i: (i, 0)),
                  pl.BlockSpec((bt, width), lambda i: (i, 0)),
                  pl.BlockSpec((bt, width), lambda i: (i, 1)),
                  pl.BlockSpec((bt, width), lambda i: (i, 2)),
                  halo(1), halo(2),
                  pl.BlockSpec((CONV_K, width), lambda i: (0, 0)),
                  pl.BlockSpec((1, width), lambda i: (0, 0)),
                  pl.BlockSpec((1, width), lambda i: (0, 0))],
        out_specs=pl.BlockSpec((bt, 2 * width), lambda i: (i, 0)),
        compiler_params=_cparams(("arbitrary",)),
        name="conv_gate_norm",
    )(y_attn, gcu, gcu, gcu, gcu, gcu, conv_w,
      g_attn.reshape(1, width), g_conv.reshape(1, width))


def _res1_kernel(x_ref, h_ref, mod_ref, gpost_ref, gpre_ref, x1_ref, h2_ref):
    x1 = x_ref[...] + mod_ref[2:3, :] * _rms(h_ref[...], gpost_ref[...])
    x1_ref[...] = x1
    y = _rms(x1, gpre_ref[...])
    h2_ref[...] = (y * (1.0 + mod_ref[4:5, :]) + mod_ref[3:4, :]).astype(h2_ref.dtype)


def _res1(x2, h, mod3, g_post, g_pre, seq, bt=256):
    t, d = x2.shape
    row = pl.BlockSpec((bt, d), lambda i: (i, 0))
    vec = pl.BlockSpec((1, d), lambda i: (0, 0))
    return pl.pallas_call(
        _res1_kernel,
        out_shape=(jax.ShapeDtypeStruct((t, d), F32), jax.ShapeDtypeStruct((t, d), BF16)),
        grid=(t // bt,),
        in_specs=[row, row,
                  pl.BlockSpec((None, N_MOD, d), lambda i: ((i * bt) // seq, 0, 0)),
                  vec, vec],
        out_specs=(row, row),
        compiler_params=_cparams(("arbitrary",)),
        name="residual_norm_mod",
    )(x2, h, mod3, g_post.reshape(1, d), g_pre.reshape(1, d))


def _res2_kernel(x_ref, h_ref, mod_ref, gpost_ref, o_ref):
    o_ref[...] = x_ref[...] + mod_ref[5:6, :] * _rms(h_ref[...], gpost_ref[...])


def _res2(x1, h, mod3, g_post, seq, bt=256):
    t, d = x1.shape
    row = pl.BlockSpec((bt, d), lambda i: (i, 0))
    return pl.pallas_call(
        _res2_kernel,
        out_shape=jax.ShapeDtypeStruct((t, d), F32),
        grid=(t // bt,),
        in_specs=[row, row,
                  pl.BlockSpec((None, N_MOD, d), lambda i: ((i * bt) // seq, 0, 0)),
                  pl.BlockSpec((1, d), lambda i: (0, 0))],
        out_specs=row,
        compiler_params=_cparams(("arbitrary",)),
        name="residual_norm",
    )(x1, h, mod3, g_post.reshape(1, d))


def _ffn_up_kernel(x_ref, wg_ref, wu_ref, o_ref):
    x = x_ref[...]
    g = jnp.dot(x, wg_ref[...].astype(BF16), preferred_element_type=F32)
    u = jnp.dot(x, wu_ref[...].astype(BF16), preferred_element_type=F32)
    o_ref[...] = (g * jax.nn.sigmoid(g) * u).astype(o_ref.dtype)


def _ffn_up(x, wg, wu, layer, bm=2048, bn=256):
    m, k = x.shape
    n = wg.shape[2]
    assert m % bm == 0 and n % bn == 0
    wspec = pl.BlockSpec((None, k, bn), lambda i, j: (layer, 0, j))
    xspec = pl.BlockSpec((bm, k), lambda i, j: (i, 0), pipeline_mode=pl.Buffered(1))
    return pl.pallas_call(
        _ffn_up_kernel,
        out_shape=jax.ShapeDtypeStruct((m, n), BF16),
        grid=(m // bm, n // bn),
        in_specs=[xspec, wspec, wspec],
        out_specs=pl.BlockSpec((bm, bn), lambda i, j: (i, j)),
        compiler_params=_cparams(("arbitrary", "arbitrary")),
        name="swiglu_up",
    )(x, wg, wu)


def kernel(x, c, w_ada, b_ada, pre_norm_mix, w_in, b_f, conv_w, attn_out_norm,
           conv_out_norm, w_out, post_norm_mix, pre_norm_ffn, w_gate, w_up, w_down,
           post_norm_ffn):
    bsz, seq, d = x.shape
    depth = w_ada.shape[0]
    n_heads = b_f.shape[1]
    attn_w = n_heads * HEAD_DIM
    conv_wd = conv_w.shape[2]
    t = bsz * seq
    lane = 128

    x2 = x.reshape(t, d)
    for l in range(depth):
        w_in_nk = jnp.transpose(w_in[l]).astype(BF16)
        f0 = 3 * attn_w
        w_f_nk = jnp.pad(w_in_nk[f0:f0 + n_heads], ((0, lane - n_heads), (0, 0)))
        w_gcu_nk = w_in_nk[f0 + n_heads:]
        w_o = w_out[l].astype(BF16)
        w_d = w_down[l].astype(BF16)
        b_pad = jnp.pad(b_f[l], (0, lane - n_heads)).reshape(1, lane)

        mod3 = _ada(c, w_ada[l], b_ada[l]).reshape(bsz, N_MOD, d)

        h1 = _prenorm(x2, mod3, pre_norm_mix[l], seq)
        qkv = _mm(h1, w_in_nk, 3 * attn_w, BF16, 1024, 1024, "in_proj_qkv", w_is_nk=True,
                  scaled_cols=attn_w, scale=HEAD_DIM ** -0.5 * LOG2E)
        gcu = _mm(h1, w_gcu_nk, 3 * conv_wd, F32, 1024, 1024, "in_proj_conv", w_is_nk=True)
        f_logit = _mm(h1, w_f_nk, lane, F32, 1024, lane, "in_proj_forget", w_is_nk=True)
        f_cum = _fcum(f_logit.reshape(bsz, seq, lane), b_pad)
        y_attn = _attention(qkv.reshape(bsz, seq, 3 * attn_w), f_cum, n_heads)
        y_cat = _mixpost(y_attn.reshape(t, attn_w), gcu, conv_w[l],
                         attn_out_norm[l], conv_out_norm[l], seq)
        h = _mm(y_cat, w_o, d, F32, 1024, 1024, "out_proj")
        x2, h2 = _res1(x2, h, mod3, post_norm_mix[l], pre_norm_ffn[l], seq)

        a = _ffn_up(h2, w_gate, w_up, l)
        h3 = _mm(a, w_d, d, F32, 512, 512, "swiglu_down")
        x2 = _res2(x2, h3, mod3, post_norm_ffn[l], seq)
    return x2.reshape(bsz, seq, d)
```

```python
import functools

import jax
import jax.numpy as jnp
from jax import lax
from jax.experimental import pallas as pl
from jax.experimental.pallas import tpu as pltpu

HEAD_DIM = 128
CONV_K = 3
EPS = 1e-6
N_MOD = 6
V7X_VMEM_BYTES = 64 * 1024 * 1024
VMEM_LIMIT = V7X_VMEM_BYTES - 4 * 1024 * 1024
SUBLANES = 8

F32 = jnp.float32
BF16 = jnp.bfloat16


def _cparams(sem):
    return pltpu.CompilerParams(dimension_semantics=sem, vmem_limit_bytes=VMEM_LIMIT)


def _rms(x, g):
    return x * lax.rsqrt(jnp.mean(x * x, axis=-1, keepdims=True) + EPS) * g


def _ada_kernel(c_ref, w_ref, b_ref, o_ref):
    c = c_ref[...]
    c_act = (c * jax.nn.sigmoid(c)).astype(BF16)
    o_ref[...] = jnp.dot(c_act, w_ref[...].astype(BF16),
                         preferred_element_type=F32) + b_ref[...]


def _ada(c, w, b, bn=512):
    bsz, d = c.shape
    n = w.shape[1]
    return pl.pallas_call(
        _ada_kernel,
        out_shape=jax.ShapeDtypeStruct((bsz, n), F32),
        grid=(n // bn,),
        in_specs=[pl.BlockSpec((bsz, d), lambda j: (0, 0)),
                  pl.BlockSpec((d, bn), lambda j: (0, j)),
                  pl.BlockSpec((1, bn), lambda j: (0, j))],
        out_specs=pl.BlockSpec((bsz, bn), lambda j: (0, j)),
        compiler_params=_cparams(("arbitrary",)),
        name="ada_mod",
    )(c, w, b.reshape(1, n))


def _prenorm_kernel(x_ref, mod_ref, g_ref, o_ref):
    y = _rms(x_ref[...], g_ref[...])
    o_ref[...] = (y * (1.0 + mod_ref[1:2, :]) + mod_ref[0:1, :]).astype(o_ref.dtype)


def _prenorm(x2, mod3, g, seq, bt=512):
    t, d = x2.shape
    return pl.pallas_call(
        _prenorm_kernel,
        out_shape=jax.ShapeDtypeStruct((t, d), BF16),
        grid=(t // bt,),
        in_specs=[pl.BlockSpec((bt, d), lambda i: (i, 0)),
                  pl.BlockSpec((None, N_MOD, d), lambda i: ((i * bt) // seq, 0, 0)),
                  pl.BlockSpec((1, d), lambda i: (0, 0))],
        out_specs=pl.BlockSpec((bt, d), lambda i: (i, 0)),
        compiler_params=_cparams(("arbitrary",)),
        name="prenorm_mod",
    )(x2, mod3, g.reshape(1, d))


def _mm_kernel(x_ref, w_ref, o_ref, *, w_is_nk, scaled_blocks, scale):
    contract_w = 1 if w_is_nk else 0
    acc = lax.dot_general(x_ref[...], w_ref[...], (((1,), (contract_w,)), ((), ())),
                          preferred_element_type=F32)
    if scaled_blocks:
        acc = acc * jnp.where(pl.program_id(1) < scaled_blocks, scale, 1.0)
    o_ref[...] = acc.astype(o_ref.dtype)


def _mm(x, w, n, out_dtype, bm, bn, name, w_is_nk=False, scaled_cols=0, scale=1.0):
    m, k = x.shape
    assert m % bm == 0 and n % bn == 0 and scaled_cols % bn == 0
    if w_is_nk:
        w_spec = pl.BlockSpec((bn, k), lambda i, j: (j, 0))
    else:
        w_spec = pl.BlockSpec((k, bn), lambda i, j: (0, j))
    return pl.pallas_call(
        functools.partial(_mm_kernel, w_is_nk=w_is_nk,
                          scaled_blocks=scaled_cols // bn, scale=scale),
        out_shape=jax.ShapeDtypeStruct((m, n), out_dtype),
        grid=(m // bm, n // bn),
        in_specs=[pl.BlockSpec((bm, k), lambda i, j: (i, 0)), w_spec],
        out_specs=pl.BlockSpec((bm, bn), lambda i, j: (i, j)),
        compiler_params=_cparams(("arbitrary", "arbitrary")),
        name=name,
    )(x, w)


LOG2E = 1.4426950408889634
AUG_LANES = 8
N_SPLIT = 3


def _fcum_kernel(f_ref, b_ref, qaug_ref, kaug_ref, carry_ref, *, bs, n_heads):
    @pl.when(pl.program_id(1) == 0)
    def _():
        carry_ref[...] = jnp.zeros_like(carry_ref)

    x = f_ref[...] + b_ref[...]
    log_f = jnp.minimum(x, 0.0) - jnp.log1p(jnp.exp(-jnp.abs(x)))
    row = lax.broadcasted_iota(jnp.int32, (bs, bs), 0)
    col = lax.broadcasted_iota(jnp.int32, (bs, bs), 1)
    tril = (col <= row).astype(F32)
    cum = jnp.dot(tril, log_f, preferred_element_type=F32,
                  precision=lax.Precision.HIGHEST) + carry_ref[...]
    carry_ref[...] = cum[bs - 1:bs, :]

    n = cum.shape[1]
    x2 = cum * LOG2E
    hi = x2.astype(BF16)
    r = x2 - hi.astype(F32)
    mid = r.astype(BF16)
    lo = (r - mid.astype(F32)).astype(BF16)
    src = lax.broadcasted_iota(jnp.int32, (n, n), 0)
    dst = lax.broadcasted_iota(jnp.int32, (n, n), 1)

    def place(piece, c):
        sel = ((dst == AUG_LANES * src + c) & (src < n_heads)).astype(BF16)
        return jnp.dot(piece, sel, preferred_element_type=F32)

    slot = lax.broadcasted_iota(jnp.int32, (bs, n), 1) % AUG_LANES
    q_ones = ((slot >= N_SPLIT) & (slot < 2 * N_SPLIT)).astype(F32)
    k_ones = (slot < N_SPLIT).astype(F32)
    qaug_ref[...] = (place(hi, 0) + place(mid, 1) + place(lo, 2) + q_ones).astype(BF16)
    kaug_ref[...] = (k_ones - (place(hi, N_SPLIT) + place(mid, N_SPLIT + 1)
                               + place(lo, N_SPLIT + 2))).astype(BF16)


def _fcum(f3, b_pad, n_heads, bs=512):
    bsz, seq, n = f3.shape
    assert n_heads * AUG_LANES <= n
    blk = pl.BlockSpec((None, bs, n), lambda b, i: (b, i, 0))
    return pl.pallas_call(
        functools.partial(_fcum_kernel, bs=bs, n_heads=n_heads),
        out_shape=(jax.ShapeDtypeStruct((bsz, seq, n), BF16),) * 2,
        grid=(bsz, seq // bs),
        in_specs=[blk, pl.BlockSpec((1, n), lambda b, i: (0, 0))],
        out_specs=(blk, blk),
        scratch_shapes=[pltpu.VMEM((1, n), F32)],
        compiler_params=_cparams(("arbitrary", "arbitrary")),
        name="forget_cumsum",
    )(f3, b_pad)


def _attn_kernel(q_ref, k_ref, v_ref, qaug_ref, kaug_ref, o_ref,
                 q2_sc, k2_sc, v2_sc, sa_sc, sb_sc, m_sc, acc_sc, *, blk, seq, chunk):
    h = pl.program_id(1)
    d = HEAD_DIM
    n_tiles = seq // blk
    n_pairs = n_tiles * (n_tiles + 1) // 2
    assert n_pairs % 2 == 0 and n_pairs >= 2

    def build(c, carry):
        r = pl.multiple_of(c * chunk, chunk)
        owner = lax.broadcasted_iota(jnp.int32, (chunk, d), 1) // AUG_LANES
        q2_sc[pl.ds(r, chunk), :d] = q_ref[pl.ds(r, chunk), :]
        q2_sc[pl.ds(r, chunk), d:] = qaug_ref[pl.ds(r, chunk), :]
        k2_sc[pl.ds(r, chunk), :d] = k_ref[pl.ds(r, chunk), :]
        k2_sc[pl.ds(r, chunk), d:] = jnp.where(owner == h, kaug_ref[pl.ds(r, chunk), :],
                                               jnp.zeros((), BF16))
        v2_sc[pl.ds(r, chunk), :d] = v_ref[pl.ds(r, chunk), :]
        v2_sc[pl.ds(r, chunk), d:] = jnp.ones((chunk, d), BF16)
        return carry

    lax.fori_loop(0, seq // chunk, build, 0)
    m_sc[...] = jnp.full_like(m_sc, -jnp.inf)
    acc_sc[...] = jnp.zeros_like(acc_sc)

    def scores(qi, j, s_ref):
        q0 = pl.multiple_of(qi * blk, blk)
        k0 = pl.multiple_of(j * blk, blk)
        s_ref[...] = lax.dot_general(q2_sc[pl.ds(q0, blk), :], k2_sc[pl.ds(k0, blk), :],
                                     (((1,), (1,)), ((), ())),
                                     preferred_element_type=F32)

    def update(qi, j, s_ref, diagonal):
        k0 = pl.multiple_of(j * blk, blk)
        s = s_ref[...]
        if diagonal:
            col = lax.broadcasted_iota(jnp.int32, (blk, blk), 1)
            row = lax.broadcasted_iota(jnp.int32, (blk, blk), 0)
            s = jnp.where(col <= row, s, -jnp.inf)
        m_prev = jnp.where(j == 0, -jnp.inf, m_sc[...])
        m_new = jnp.maximum(m_prev, jnp.max(s, axis=1, keepdims=True))
        p = jnp.exp2(s - jnp.tile(m_new, (1, blk // d)))
        alpha = jnp.exp2(m_prev - m_new)
        pv = jnp.dot(p.astype(BF16), v2_sc[pl.ds(k0, blk), :],
                     preferred_element_type=F32)
        acc = jnp.tile(alpha, (1, 2)) * acc_sc[...] + pv
        acc_sc[...] = acc
        m_sc[...] = m_new
        if diagonal:
            q0 = pl.multiple_of(qi * blk, blk)
            o_ref[pl.ds(q0, blk), :] = (acc[:, :d] / acc[:, d:]).astype(o_ref.dtype)

    def successor(qi, j):
        last = j == qi
        return jnp.where(last, qi + 1, qi), jnp.where(last, 0, j + 1)

    def stage(qi, j, s_cur, s_next, prefetch=True):
        qi1, j1 = successor(qi, j)
        for diagonal in (False, True):
            @pl.when((j == qi) if diagonal else (j != qi))
            def _():
                if prefetch:
                    scores(qi1, j1, s_next)
                update(qi, j, s_cur, diagonal)
        return qi1, j1

    scores(0, 0, sa_sc)

    def body(it, carry):
        qi, j = stage(*carry, sa_sc, sb_sc)
        return stage(qi, j, sb_sc, sa_sc)

    zero = jnp.int32(0)
    qi, j = lax.fori_loop(0, n_pairs // 2 - 1, body, (zero, zero))
    qi, j = stage(qi, j, sa_sc, sb_sc)
    stage(qi, j, sb_sc, sa_sc, prefetch=False)


def _attention(qkv3, qaug, kaug, n_heads, blk=1024, chunk=512):
    bsz, seq, _ = qkv3.shape
    d = HEAD_DIM
    assert seq % blk == 0 and seq % chunk == 0 and qaug.shape[2] == d
    kern = functools.partial(_attn_kernel, blk=blk, seq=seq, chunk=chunk)
    head_block = lambda first: pl.BlockSpec((None, seq, d), lambda b, h: (b, 0, first + h))
    aug_block = pl.BlockSpec((None, seq, d), lambda b, h: (b, 0, 0),
                             pipeline_mode=pl.Buffered(1))
    return pl.pallas_call(
        kern,
        out_shape=jax.ShapeDtypeStruct((bsz, seq, n_heads * d), F32),
        grid=(bsz, n_heads),
        in_specs=[head_block(0), head_block(n_heads), head_block(2 * n_heads),
                  aug_block, aug_block],
        out_specs=head_block(0),
        scratch_shapes=[pltpu.VMEM((seq, 2 * d), BF16),
                        pltpu.VMEM((seq, 2 * d), BF16),
                        pltpu.VMEM((seq, 2 * d), BF16),
                        pltpu.VMEM((blk, blk), F32),
                        pltpu.VMEM((blk, blk), F32),
                        pltpu.VMEM((blk, d), F32),
                        pltpu.VMEM((blk, 2 * d), F32)],
        compiler_params=_cparams(("arbitrary", "arbitrary")),
        name="fox_attention",
    )(qkv3, qkv3, qkv3, qaug, kaug)


def _conv_proj_kernel(x_ref, wb_ref, wc_ref, wu_ref, cw_ref, o_ref, carry_ref, *, bm, seq):
    i = pl.program_id(0)
    n = pl.program_id(1)
    nk = (((1,), (1,)), ((), ()))
    x = x_ref[...]
    gate_b = lax.dot_general(x, wb_ref[...], nk, preferred_element_type=F32)
    gate_c = lax.dot_general(x, wc_ref[...], nk, preferred_element_type=F32)
    u = lax.dot_general(x, wu_ref[...], nk, preferred_element_type=F32)
    z = gate_c * u

    @pl.when((i * bm) % seq == 0)
    def _():
        carry_ref[n] = jnp.zeros(carry_ref.shape[1:], F32)

    prev = carry_ref[n]
    row = lax.broadcasted_iota(jnp.int32, (bm, 1), 0)
    z1 = jnp.where(row == 0, prev[SUBLANES - 1:SUBLANES, :], pltpu.roll(z, 1, 0))
    z2 = jnp.where(row == 0, prev[SUBLANES - 2:SUBLANES - 1, :],
                   jnp.where(row == 1, prev[SUBLANES - 1:SUBLANES, :], pltpu.roll(z, 2, 0)))
    conv = cw_ref[2:3, :] * z + cw_ref[1:2, :] * z1 + cw_ref[0:1, :] * z2
    o_ref[...] = gate_b * conv
    carry_ref[n] = z[bm - SUBLANES:, :]


def _conv_proj(x, w_gcu_nk, conv_w, seq, bm=1024, bn=512):
    m, k = x.shape
    width = conv_w.shape[1]
    assert CONV_K == 3 and m % bm == 0 and width % bn == 0 and seq % bm == 0
    nb = width // bn
    wspec = lambda g: pl.BlockSpec((bn, k), lambda i, j: (g * nb + j, 0))
    return pl.pallas_call(
        functools.partial(_conv_proj_kernel, bm=bm, seq=seq),
        out_shape=jax.ShapeDtypeStruct((m, width), F32),
        grid=(m // bm, nb),
        in_specs=[pl.BlockSpec((bm, k), lambda i, j: (i, 0)),
                  wspec(0), wspec(1), wspec(2),
                  pl.BlockSpec((CONV_K, bn), lambda i, j: (0, j))],
        out_specs=pl.BlockSpec((bm, bn), lambda i, j: (i, j)),
        scratch_shapes=[pltpu.VMEM((nb, SUBLANES, bn), F32)],
        compiler_params=_cparams(("arbitrary", "arbitrary")),
        name="in_proj_conv",
    )(x, w_gcu_nk, w_gcu_nk, w_gcu_nk, conv_w)


def _mixnorm_kernel(ya_ref, yc_ref, ga_ref, gc_ref, o_ref, *, width):
    o_ref[:, :width] = _rms(ya_ref[...], ga_ref[...]).astype(o_ref.dtype)
    o_ref[:, width:] = _rms(yc_ref[...], gc_ref[...]).astype(o_ref.dtype)


def _mixnorm(y_attn, y_conv, g_attn, g_conv, bt=512):
    t, width = y_attn.shape
    row = pl.BlockSpec((bt, width), lambda i: (i, 0))
    vec = pl.BlockSpec((1, width), lambda i: (0, 0))
    return pl.pallas_call(
        functools.partial(_mixnorm_kernel, width=width),
        out_shape=jax.ShapeDtypeStruct((t, 2 * width), BF16),
        grid=(t // bt,),
        in_specs=[row, row, vec, vec],
        out_specs=pl.BlockSpec((bt, 2 * width), lambda i: (i, 0)),
        compiler_params=_cparams(("arbitrary",)),
        name="mixer_norm",
    )(y_attn, y_conv, g_attn.reshape(1, width), g_conv.reshape(1, width))


def _res1_kernel(x_ref, h_ref, mod_ref, gpost_ref, gpre_ref, x1_ref, h2_ref):
    x1 = x_ref[...] + mod_ref[2:3, :] * _rms(h_ref[...], gpost_ref[...])
    x1_ref[...] = x1
    y = _rms(x1, gpre_ref[...])
    h2_ref[...] = (y * (1.0 + mod_ref[4:5, :]) + mod_ref[3:4, :]).astype(h2_ref.dtype)


def _res1(x2, h, mod3, g_post, g_pre, seq, bt=256):
    t, d = x2.shape
    row = pl.BlockSpec((bt, d), lambda i: (i, 0))
    vec = pl.BlockSpec((1, d), lambda i: (0, 0))
    return pl.pallas_call(
        _res1_kernel,
        out_shape=(jax.ShapeDtypeStruct((t, d), F32), jax.ShapeDtypeStruct((t, d), BF16)),
        grid=(t // bt,),
        in_specs=[row, row,
                  pl.BlockSpec((None, N_MOD, d), lambda i: ((i * bt) // seq, 0, 0)),
                  vec, vec],
        out_specs=(row, row),
        compiler_params=_cparams(("arbitrary",)),
        name="residual_norm_mod",
    )(x2, h, mod3, g_post.reshape(1, d), g_pre.reshape(1, d))


def _res2_kernel(x_ref, h_ref, mod_ref, gpost_ref, o_ref):
    o_ref[...] = x_ref[...] + mod_ref[5:6, :] * _rms(h_ref[...], gpost_ref[...])


def _res2(x1, h, mod3, g_post, seq, bt=256):
    t, d = x1.shape
    row = pl.BlockSpec((bt, d), lambda i: (i, 0))
    return pl.pallas_call(
        _res2_kernel,
        out_shape=jax.ShapeDtypeStruct((t, d), F32),
        grid=(t // bt,),
        in_specs=[row, row,
                  pl.BlockSpec((None, N_MOD, d), lambda i: ((i * bt) // seq, 0, 0)),
                  pl.BlockSpec((1, d), lambda i: (0, 0))],
        out_specs=row,
        compiler_params=_cparams(("arbitrary",)),
        name="residual_norm",
    )(x1, h, mod3, g_post.reshape(1, d))


def _ffn_up_kernel(x_ref, wg_ref, wu_ref, o_ref):
    x = x_ref[...]
    g = jnp.dot(x, wg_ref[...].astype(BF16), preferred_element_type=F32)
    u = jnp.dot(x, wu_ref[...].astype(BF16), preferred_element_type=F32)
    o_ref[...] = (g * jax.nn.sigmoid(g) * u).astype(o_ref.dtype)


def _ffn_up(x, wg, wu, layer, bm=2048, bn=256):
    m, k = x.shape
    n = wg.shape[2]
    assert m % bm == 0 and n % bn == 0
    wspec = pl.BlockSpec((None, k, bn), lambda i, j: (layer, 0, j))
    xspec = pl.BlockSpec((bm, k), lambda i, j: (i, 0), pipeline_mode=pl.Buffered(1))
    return pl.pallas_call(
        _ffn_up_kernel,
        out_shape=jax.ShapeDtypeStruct((m, n), BF16),
        grid=(m // bm, n // bn),
        in_specs=[xspec, wspec, wspec],
        out_specs=pl.BlockSpec((bm, bn), lambda i, j: (i, j)),
        compiler_params=_cparams(("arbitrary", "arbitrary")),
        name="swiglu_up",
    )(x, wg, wu)


def kernel(x, c, w_ada, b_ada, pre_norm_mix, w_in, b_f, conv_w, attn_out_norm,
           conv_out_norm, w_out, post_norm_mix, pre_norm_ffn, w_gate, w_up, w_down,
           post_norm_ffn):
    bsz, seq, d = x.shape
    depth = w_ada.shape[0]
    n_heads = b_f.shape[1]
    attn_w = n_heads * HEAD_DIM
    conv_wd = conv_w.shape[2]
    t = bsz * seq
    lane = 128

    x2 = x.reshape(t, d)
    for l in range(depth):
        w_in_nk = jnp.transpose(w_in[l]).astype(BF16)
        f0 = 3 * attn_w
        w_f_nk = jnp.pad(w_in_nk[f0:f0 + n_heads], ((0, lane - n_heads), (0, 0)))
        w_gcu_nk = w_in_nk[f0 + n_heads:]
        w_o = w_out[l].astype(BF16)
        w_d = w_down[l].astype(BF16)
        b_pad = jnp.pad(b_f[l], (0, lane - n_heads)).reshape(1, lane)

        mod3 = _ada(c, w_ada[l], b_ada[l]).reshape(bsz, N_MOD, d)

        h1 = _prenorm(x2, mod3, pre_norm_mix[l], seq)
        qkv = _mm(h1, w_in_nk, 3 * attn_w, BF16, 1024, 1024, "in_proj_qkv", w_is_nk=True,
                  scaled_cols=attn_w, scale=HEAD_DIM ** -0.5 * LOG2E)
        y_conv = _conv_proj(h1, w_gcu_nk, conv_w[l], seq)
        f_logit = _mm(h1, w_f_nk, lane, F32, 1024, lane, "in_proj_forget", w_is_nk=True)
        qaug, kaug = _fcum(f_logit.reshape(bsz, seq, lane), b_pad, n_heads)
        y_attn = _attention(qkv.reshape(bsz, seq, 3 * attn_w), qaug, kaug, n_heads)
        y_cat = _mixnorm(y_attn.reshape(t, attn_w), y_conv,
                         attn_out_norm[l], conv_out_norm[l])
        h = _mm(y_cat, w_o, d, F32, 1024, 1024, "out_proj")
        x2, h2 = _res1(x2, h, mod3, post_norm_mix[l], pre_norm_ffn[l], seq)

        a = _ffn_up(h2, w_gate, w_up, l)
        h3 = _mm(a, w_d, d, F32, 512, 512, "swiglu_down")
        x2 = _res2(x2, h3, mod3, post_norm_ffn[l], seq)
    return x2.reshape(bsz, seq, d)
```

```python
import functools

import jax
import jax.numpy as jnp
from jax import lax
from jax.experimental import pallas as pl
from jax.experimental.pallas import tpu as pltpu

HEAD_DIM = 128
CONV_K = 3
EPS = 1e-6
N_MOD = 6
V7X_VMEM_BYTES = 64 * 1024 * 1024
VMEM_LIMIT = V7X_VMEM_BYTES - 4 * 1024 * 1024
SUBLANES = 8

F32 = jnp.float32
BF16 = jnp.bfloat16


def _cparams(sem):
    return pltpu.CompilerParams(dimension_semantics=sem, vmem_limit_bytes=VMEM_LIMIT)


def _rms(x, g):
    return x * lax.rsqrt(jnp.mean(x * x, axis=-1, keepdims=True) + EPS) * g


def _ada_kernel(c_ref, w_ref, b_ref, o_ref):
    c = c_ref[...]
    c_act = (c * jax.nn.sigmoid(c)).astype(BF16)
    o_ref[...] = jnp.dot(c_act, w_ref[...].astype(BF16),
                         preferred_element_type=F32) + b_ref[...]


def _ada(c, w3, b2, layer, n, bn=512):
    bsz, d = c.shape
    assert n % bn == 0
    return pl.pallas_call(
        _ada_kernel,
        out_shape=jax.ShapeDtypeStruct((bsz, n), F32),
        grid=(n // bn,),
        in_specs=[pl.BlockSpec((bsz, d), lambda j: (0, 0)),
                  pl.BlockSpec((None, d, bn), lambda j: (layer, 0, j)),
                  pl.BlockSpec((None, 1, bn), lambda j: (layer, 0, j))],
        out_specs=pl.BlockSpec((bsz, bn), lambda j: (0, j)),
        compiler_params=_cparams(("arbitrary",)),
        name="ada_mod",
    )(c, w3, b2[:, None, :])


def _prenorm_kernel(x_ref, mod_ref, g_ref, o_ref):
    y = _rms(x_ref[...], g_ref[...])
    o_ref[...] = (y * (1.0 + mod_ref[1:2, :]) + mod_ref[0:1, :]).astype(o_ref.dtype)


def _prenorm(x2, mod, g, seq, bt=512):
    t, d = x2.shape
    return pl.pallas_call(
        _prenorm_kernel,
        out_shape=jax.ShapeDtypeStruct((t, d), BF16),
        grid=(t // bt,),
        in_specs=[pl.BlockSpec((bt, d), lambda i: (i, 0)),
                  pl.BlockSpec((None,) + mod.shape[1:], lambda i: ((i * bt) // seq, 0, 0)),
                  pl.BlockSpec((1, d), lambda i: (0, 0))],
        out_specs=pl.BlockSpec((bt, d), lambda i: (i, 0)),
        compiler_params=_cparams(("arbitrary",)),
        name="prenorm_mod",
    )(x2, mod, g.reshape(1, d))


def _mm_kernel(x_ref, w_ref, o_ref, *, w_is_nk, scaled_blocks, scale):
    contract_w = 1 if w_is_nk else 0
    acc = lax.dot_general(x_ref[...], w_ref[...], (((1,), (contract_w,)), ((), ())),
                          preferred_element_type=F32)
    if scaled_blocks:
        acc = acc * jnp.where(pl.program_id(1) < scaled_blocks, scale, 1.0)
    o_ref[...] = acc.astype(o_ref.dtype)


def _mm(x, w, n, out_dtype, bm, bn, name, w_is_nk=False, scaled_cols=0, scale=1.0):
    m, k = x.shape
    assert m % bm == 0 and n % bn == 0 and scaled_cols % bn == 0
    if w_is_nk:
        w_spec = pl.BlockSpec((bn, k), lambda i, j: (j, 0))
    else:
        w_spec = pl.BlockSpec((k, bn), lambda i, j: (0, j))
    return pl.pallas_call(
        functools.partial(_mm_kernel, w_is_nk=w_is_nk,
                          scaled_blocks=scaled_cols // bn, scale=scale),
        out_shape=jax.ShapeDtypeStruct((m, n), out_dtype),
        grid=(m // bm, n // bn),
        in_specs=[pl.BlockSpec((bm, k), lambda i, j: (i, 0)), w_spec],
        out_specs=pl.BlockSpec((bm, bn), lambda i, j: (i, j)),
        compiler_params=_cparams(("arbitrary", "arbitrary")),
        name=name,
    )(x, w)


LOG2E = 1.4426950408889634
AUG_LANES = 8
N_SPLIT = 3


def _fcum_kernel(f_ref, b_ref, qaug_ref, kaug_ref, carry_ref, *, bs, n_heads):
    @pl.when(pl.program_id(1) == 0)
    def _():
        carry_ref[...] = jnp.zeros_like(carry_ref)

    x = f_ref[...] + b_ref[...]
    log_f = jnp.minimum(x, 0.0) - jnp.log1p(jnp.exp(-jnp.abs(x)))
    row = lax.broadcasted_iota(jnp.int32, (bs, bs), 0)
    col = lax.broadcasted_iota(jnp.int32, (bs, bs), 1)
    tril = (col <= row).astype(F32)
    cum = jnp.dot(tril, log_f, preferred_element_type=F32,
                  precision=lax.Precision.HIGHEST) + carry_ref[...]
    carry_ref[...] = cum[bs - 1:bs, :]

    n = cum.shape[1]
    x2 = cum * LOG2E
    hi = x2.astype(BF16)
    r = x2 - hi.astype(F32)
    mid = r.astype(BF16)
    lo = (r - mid.astype(F32)).astype(BF16)
    src = lax.broadcasted_iota(jnp.int32, (n, n), 0)
    dst = lax.broadcasted_iota(jnp.int32, (n, n), 1)

    def place(piece, c):
        sel = ((dst == AUG_LANES * src + c) & (src < n_heads)).astype(BF16)
        return jnp.dot(piece, sel, preferred_element_type=F32)

    slot = lax.broadcasted_iota(jnp.int32, (bs, n), 1) % AUG_LANES
    q_ones = ((slot >= N_SPLIT) & (slot < 2 * N_SPLIT)).astype(F32)
    k_ones = (slot < N_SPLIT).astype(F32)
    qaug_ref[...] = (place(hi, 0) + place(mid, 1) + place(lo, 2) + q_ones).astype(BF16)
    kaug_ref[...] = (k_ones - (place(hi, N_SPLIT) + place(mid, N_SPLIT + 1)
                               + place(lo, N_SPLIT + 2))).astype(BF16)


def _fcum(f3, b_pad, n_heads, bs=512):
    bsz, seq, n = f3.shape
    assert n_heads * AUG_LANES <= n
    blk = pl.BlockSpec((None, bs, n), lambda b, i: (b, i, 0))
    return pl.pallas_call(
        functools.partial(_fcum_kernel, bs=bs, n_heads=n_heads),
        out_shape=(jax.ShapeDtypeStruct((bsz, seq, n), BF16),) * 2,
        grid=(bsz, seq // bs),
        in_specs=[blk, pl.BlockSpec((1, n), lambda b, i: (0, 0))],
        out_specs=(blk, blk),
        scratch_shapes=[pltpu.VMEM((1, n), F32)],
        compiler_params=_cparams(("arbitrary", "arbitrary")),
        name="forget_cumsum",
    )(f3, b_pad)


def _attn_kernel(q_ref, k_ref, v_ref, qaug_ref, kaug_ref, o_ref,
                 q2_sc, k2_sc, v2_sc, sa_sc, sb_sc, m_sc, acc_sc, *, blk, seq, chunk):
    h = pl.program_id(1)
    d = HEAD_DIM
    n_tiles = seq // blk
    n_pairs = n_tiles * (n_tiles + 1) // 2
    assert n_pairs % 2 == 0 and n_pairs >= 2

    def build(c, carry):
        r = pl.multiple_of(c * chunk, chunk)
        owner = lax.broadcasted_iota(jnp.int32, (chunk, d), 1) // AUG_LANES
        q2_sc[pl.ds(r, chunk), :d] = q_ref[pl.ds(r, chunk), :]
        q2_sc[pl.ds(r, chunk), d:] = qaug_ref[pl.ds(r, chunk), :]
        k2_sc[pl.ds(r, chunk), :d] = k_ref[pl.ds(r, chunk), :]
        k2_sc[pl.ds(r, chunk), d:] = jnp.where(owner == h, kaug_ref[pl.ds(r, chunk), :],
                                               jnp.zeros((), BF16))
        v2_sc[pl.ds(r, chunk), :d] = v_ref[pl.ds(r, chunk), :]
        v2_sc[pl.ds(r, chunk), d:] = jnp.ones((chunk, d), BF16)
        return carry

    lax.fori_loop(0, seq // chunk, build, 0)
    m_sc[...] = jnp.full_like(m_sc, -jnp.inf)
    acc_sc[...] = jnp.zeros_like(acc_sc)

    def scores(qi, j, s_ref):
        q0 = pl.multiple_of(qi * blk, blk)
        k0 = pl.multiple_of(j * blk, blk)
        s_ref[...] = lax.dot_general(q2_sc[pl.ds(q0, blk), :], k2_sc[pl.ds(k0, blk), :],
                                     (((1,), (1,)), ((), ())),
                                     preferred_element_type=F32)

    def update(qi, j, s_ref, diagonal):
        k0 = pl.multiple_of(j * blk, blk)
        s = s_ref[...]
        if diagonal:
            col = lax.broadcasted_iota(jnp.int32, (blk, blk), 1)
            row = lax.broadcasted_iota(jnp.int32, (blk, blk), 0)
            s = jnp.where(col <= row, s, -jnp.inf)
        m_prev = jnp.where(j == 0, -jnp.inf, m_sc[...])
        m_new = jnp.maximum(m_prev, jnp.max(s, axis=1, keepdims=True))
        p = jnp.exp2(s - jnp.tile(m_new, (1, blk // d)))
        alpha = jnp.exp2(m_prev - m_new)
        pv = jnp.dot(p.astype(BF16), v2_sc[pl.ds(k0, blk), :],
                     preferred_element_type=F32)
        acc = jnp.tile(alpha, (1, 2)) * acc_sc[...] + pv
        acc_sc[...] = acc
        m_sc[...] = m_new
        if diagonal:
            q0 = pl.multiple_of(qi * blk, blk)
            o_ref[pl.ds(q0, blk), :] = (acc[:, :d] / acc[:, d:]).astype(o_ref.dtype)

    def successor(qi, j):
        last = j == qi
        return jnp.where(last, qi + 1, qi), jnp.where(last, 0, j + 1)

    def stage(qi, j, s_cur, s_next, prefetch=True):
        qi1, j1 = successor(qi, j)
        for diagonal in (False, True):
            @pl.when((j == qi) if diagonal else (j != qi))
            def _():
                if prefetch:
                    scores(qi1, j1, s_next)
                update(qi, j, s_cur, diagonal)
        return qi1, j1

    scores(0, 0, sa_sc)

    def body(it, carry):
        qi, j = stage(*carry, sa_sc, sb_sc)
        return stage(qi, j, sb_sc, sa_sc)

    zero = jnp.int32(0)
    qi, j = lax.fori_loop(0, n_pairs // 2 - 1, body, (zero, zero))
    qi, j = stage(qi, j, sa_sc, sb_sc)
    stage(qi, j, sb_sc, sa_sc, prefetch=False)


def _attention(qkv3, qaug, kaug, n_heads, blk=1024, chunk=512):
    bsz, seq, _ = qkv3.shape
    d = HEAD_DIM
    assert seq % blk == 0 and seq % chunk == 0 and qaug.shape[2] == d
    kern = functools.partial(_attn_kernel, blk=blk, seq=seq, chunk=chunk)
    head_block = lambda first: pl.BlockSpec((None, seq, d), lambda b, h: (b, 0, first + h))
    aug_block = pl.BlockSpec((None, seq, d), lambda b, h: (b, 0, 0),
                             pipeline_mode=pl.Buffered(1))
    return pl.pallas_call(
        kern,
        out_shape=jax.ShapeDtypeStruct((bsz, seq, n_heads * d), F32),
        grid=(bsz, n_heads),
        in_specs=[head_block(0), head_block(n_heads), head_block(2 * n_heads),
                  aug_block, aug_block],
        out_specs=head_block(0),
        scratch_shapes=[pltpu.VMEM((seq, 2 * d), BF16),
                        pltpu.VMEM((seq, 2 * d), BF16),
                        pltpu.VMEM((seq, 2 * d), BF16),
                        pltpu.VMEM((blk, blk), F32),
                        pltpu.VMEM((blk, blk), F32),
                        pltpu.VMEM((blk, d), F32),
                        pltpu.VMEM((blk, 2 * d), F32)],
        compiler_params=_cparams(("arbitrary", "arbitrary")),
        name="fox_attention",
    )(qkv3, qkv3, qkv3, qaug, kaug)


def _conv_proj_kernel(x_ref, wb_ref, wc_ref, wu_ref, cw_ref, wo_ref, o_ref, wo16_ref,
                      carry_ref, *, bm, seq):
    wo16_ref[...] = wo_ref[...].astype(BF16)
    i = pl.program_id(0)
    n = pl.program_id(1)
    nk = (((1,), (1,)), ((), ()))
    x = x_ref[...]
    gate_b = lax.dot_general(x, wb_ref[...], nk, preferred_element_type=F32)
    gate_c = lax.dot_general(x, wc_ref[...], nk, preferred_element_type=F32)
    u = lax.dot_general(x, wu_ref[...], nk, preferred_element_type=F32)
    z = gate_c * u

    @pl.when((i * bm) % seq == 0)
    def _():
        carry_ref[n] = jnp.zeros(carry_ref.shape[1:], F32)

    prev = carry_ref[n]
    row = lax.broadcasted_iota(jnp.int32, (bm, 1), 0)
    z1 = jnp.where(row == 0, prev[SUBLANES - 1:SUBLANES, :], pltpu.roll(z, 1, 0))
    z2 = jnp.where(row == 0, prev[SUBLANES - 2:SUBLANES - 1, :],
                   jnp.where(row == 1, prev[SUBLANES - 1:SUBLANES, :], pltpu.roll(z, 2, 0)))
    conv = cw_ref[2:3, :] * z + cw_ref[1:2, :] * z1 + cw_ref[0:1, :] * z2
    o_ref[...] = gate_b * conv
    carry_ref[n] = z[bm - SUBLANES:, :]


def _conv_proj(x, w_gcu_nk, conv_w, w_out, layer, seq, bm=1024, bn=512):
    m, k = x.shape
    width = conv_w.shape[1]
    assert CONV_K == 3 and m % bm == 0 and width % bn == 0 and seq % bm == 0
    nb = width // bn
    steps = (m // bm) * nb
    rows, d_out = w_out.shape[1:]
    assert rows % steps == 0
    slab = rows // steps
    wspec = lambda g: pl.BlockSpec((bn, k), lambda i, j: (g * nb + j, 0))
    return pl.pallas_call(
        functools.partial(_conv_proj_kernel, bm=bm, seq=seq),
        out_shape=(jax.ShapeDtypeStruct((m, width), F32),
                   jax.ShapeDtypeStruct((rows, d_out), BF16)),
        grid=(m // bm, nb),
        in_specs=[pl.BlockSpec((bm, k), lambda i, j: (i, 0)),
                  wspec(0), wspec(1), wspec(2),
                  pl.BlockSpec((CONV_K, bn), lambda i, j: (0, j)),
                  pl.BlockSpec((None, slab, d_out), lambda i, j: (layer, i * nb + j, 0))],
        out_specs=(pl.BlockSpec((bm, bn), lambda i, j: (i, j)),
                   pl.BlockSpec((slab, d_out), lambda i, j: (i * nb + j, 0))),
        scratch_shapes=[pltpu.VMEM((nb, SUBLANES, bn), F32)],
        compiler_params=_cparams(("arbitrary", "arbitrary")),
        name="in_proj_conv",
    )(x, w_gcu_nk, w_gcu_nk, w_gcu_nk, conv_w, w_out)


def _mixnorm_kernel(ya_ref, yc_ref, ga_ref, gc_ref, o_ref, *, width):
    o_ref[:, :width] = _rms(ya_ref[...], ga_ref[...]).astype(o_ref.dtype)
    o_ref[:, width:] = _rms(yc_ref[...], gc_ref[...]).astype(o_ref.dtype)


def _mixnorm(y_attn, y_conv, g_attn, g_conv, bt=512):
    t, width = y_attn.shape
    row = pl.BlockSpec((bt, width), lambda i: (i, 0))
    vec = pl.BlockSpec((1, width), lambda i: (0, 0))
    return pl.pallas_call(
        functools.partial(_mixnorm_kernel, width=width),
        out_shape=jax.ShapeDtypeStruct((t, 2 * width), BF16),
        grid=(t // bt,),
        in_specs=[row, row, vec, vec],
        out_specs=pl.BlockSpec((bt, 2 * width), lambda i: (i, 0)),
        compiler_params=_cparams(("arbitrary",)),
        name="mixer_norm",
    )(y_attn, y_conv, g_attn.reshape(1, width), g_conv.reshape(1, width))


def _out_proj_kernel(y_ref, w_ref, c_ref, wa_ref, ba_ref, h_ref, mod_ref, crep_sc,
                     *, bsz, nb):
    @pl.when(pl.program_id(0) * nb + pl.program_id(1) == 0)
    def _():
        c = c_ref[...]
        c_act = (c * jax.nn.sigmoid(c)).astype(BF16)
        pick = lax.broadcasted_iota(jnp.int32, (c.shape[0], 128), 0)
        for b in range(bsz):
            crep_sc[b] = lax.dot_general(c_act, (pick == b).astype(BF16),
                                         (((0,), (0,)), ((), ())),
                                         preferred_element_type=F32)

    h_ref[...] = jnp.dot(y_ref[...], w_ref[...], preferred_element_type=F32)
    wa = wa_ref[...]
    lanes = crep_sc.shape[2]
    for b in range(bsz):
        cb = crep_sc[b]
        cols = [jnp.sum(wa[:, t * lanes:(t + 1) * lanes] * cb, axis=0, keepdims=True)
                for t in range(wa.shape[1] // lanes)]
        mod_ref[b:b + 1, :] = jnp.concatenate(cols, axis=1) + ba_ref[...]


def _out_proj(y, w_o, c_pad, bsz, w_ada, b_ada, layer, col0, bm=1024, bn=1024):
    m, k = y.shape
    n = w_o.shape[1]
    d, n_mod = w_ada.shape[1:]
    nb = n // bn
    steps = (m // bm) * nb
    assert m % bm == 0 and n % bn == 0 and (n_mod - col0) % steps == 0
    slab = (n_mod - col0) // steps
    assert slab % 128 == 0 and col0 % slab == 0
    first = col0 // slab
    return pl.pallas_call(
        functools.partial(_out_proj_kernel, bsz=bsz, nb=nb),
        out_shape=(jax.ShapeDtypeStruct((m, n), F32),
                   jax.ShapeDtypeStruct((bsz, n_mod - col0), F32)),
        grid=(m // bm, nb),
        in_specs=[pl.BlockSpec((bm, k), lambda i, j: (i, 0)),
                  pl.BlockSpec((k, bn), lambda i, j: (0, j)),
                  pl.BlockSpec(c_pad.shape, lambda i, j: (0, 0)),
                  pl.BlockSpec((None, d, slab), lambda i, j: (layer, 0, first + i * nb + j)),
                  pl.BlockSpec((None, 1, slab),
                               lambda i, j: (layer, 0, first + i * nb + j))],
        out_specs=(pl.BlockSpec((bm, bn), lambda i, j: (i, j)),
                   pl.BlockSpec((bsz, slab), lambda i, j: (0, i * nb + j))),
        scratch_shapes=[pltpu.VMEM((bsz, d, 128), F32)],
        compiler_params=_cparams(("arbitrary", "arbitrary")),
        name="out_proj",
    )(y, w_o, c_pad, w_ada, b_ada[:, None, :])


def _res1_kernel(x_ref, h_ref, mod_ref, gpost_ref, gpre_ref, x1_ref, h2_ref):
    x1 = x_ref[...] + mod_ref[2:3, :] * _rms(h_ref[...], gpost_ref[...])
    x1_ref[...] = x1
    y = _rms(x1, gpre_ref[...])
    h2_ref[...] = (y * (1.0 + mod_ref[4:5, :]) + mod_ref[3:4, :]).astype(h2_ref.dtype)


def _res1(x2, h, mod3, g_post, g_pre, seq, bt=256):
    t, d = x2.shape
    row = pl.BlockSpec((bt, d), lambda i: (i, 0))
    vec = pl.BlockSpec((1, d), lambda i: (0, 0))
    return pl.pallas_call(
        _res1_kernel,
        out_shape=(jax.ShapeDtypeStruct((t, d), F32), jax.ShapeDtypeStruct((t, d), BF16)),
        grid=(t // bt,),
        in_specs=[row, row,
                  pl.BlockSpec((None, N_MOD, d), lambda i: ((i * bt) // seq, 0, 0)),
                  vec, vec],
        out_specs=(row, row),
        compiler_params=_cparams(("arbitrary",)),
        name="residual_norm_mod",
    )(x2, h, mod3, g_post.reshape(1, d), g_pre.reshape(1, d))


def _res2_kernel(x_ref, h_ref, mod_ref, gpost_ref, o_ref):
    o_ref[...] = x_ref[...] + mod_ref[5:6, :] * _rms(h_ref[...], gpost_ref[...])


def _res2(x1, h, mod3, g_post, seq, bt=256):
    t, d = x1.shape
    row = pl.BlockSpec((bt, d), lambda i: (i, 0))
    return pl.pallas_call(
        _res2_kernel,
        out_shape=jax.ShapeDtypeStruct((t, d), F32),
        grid=(t // bt,),
        in_specs=[row, row,
                  pl.BlockSpec((None, N_MOD, d), lambda i: ((i * bt) // seq, 0, 0)),
                  pl.BlockSpec((1, d), lambda i: (0, 0))],
        out_specs=row,
        compiler_params=_cparams(("arbitrary",)),
        name="residual_norm",
    )(x1, h, mod3, g_post.reshape(1, d))


def _ffn_up_kernel(x_ref, wg_ref, wu_ref, wd_ref, o_ref, wd16_ref):
    x = x_ref[...]
    g = jnp.dot(x, wg_ref[...].astype(BF16), preferred_element_type=F32)
    u = jnp.dot(x, wu_ref[...].astype(BF16), preferred_element_type=F32)
    o_ref[...] = (g * jax.nn.sigmoid(g) * u).astype(o_ref.dtype)
    wd16_ref[...] = wd_ref[...].astype(BF16)


def _ffn_up(x, wg, wu, wd, layer, bm=2048, bn=256):
    m, k = x.shape
    n = wg.shape[2]
    assert m % bm == 0 and n % bn == 0
    steps = (m // bm) * (n // bn)
    rows, d_out = wd.shape[1:]
    assert rows % steps == 0
    slab = rows // steps
    wspec = pl.BlockSpec((None, k, bn), lambda i, j: (layer, 0, j))
    xspec = pl.BlockSpec((bm, k), lambda i, j: (i, 0), pipeline_mode=pl.Buffered(1))
    nb = n // bn
    return pl.pallas_call(
        _ffn_up_kernel,
        out_shape=(jax.ShapeDtypeStruct((m, n), BF16),
                   jax.ShapeDtypeStruct((rows, d_out), BF16)),
        grid=(m // bm, nb),
        in_specs=[xspec, wspec, wspec,
                  pl.BlockSpec((None, slab, d_out), lambda i, j: (layer, i * nb + j, 0))],
        out_specs=(pl.BlockSpec((bm, bn), lambda i, j: (i, j)),
                   pl.BlockSpec((slab, d_out), lambda i, j: (i * nb + j, 0))),
        compiler_params=_cparams(("arbitrary", "arbitrary")),
        name="swiglu_up",
    )(x, wg, wu, wd)


def kernel(x, c, w_ada, b_ada, pre_norm_mix, w_in, b_f, conv_w, attn_out_norm,
           conv_out_norm, w_out, post_norm_mix, pre_norm_ffn, w_gate, w_up, w_down,
           post_norm_ffn):
    bsz, seq, d = x.shape
    depth = w_ada.shape[0]
    n_heads = b_f.shape[1]
    attn_w = n_heads * HEAD_DIM
    t = bsz * seq
    lane = 128

    x2 = x.reshape(t, d)
    c_pad = jnp.pad(c, ((0, SUBLANES - bsz), (0, 0)))
    for l in range(depth):
        w_in_nk = jnp.transpose(w_in[l]).astype(BF16)
        f0 = 3 * attn_w
        w_f_nk = jnp.pad(w_in_nk[f0:f0 + n_heads], ((0, lane - n_heads), (0, 0)))
        w_gcu_nk = w_in_nk[f0 + n_heads:]
        b_pad = jnp.pad(b_f[l], (0, lane - n_heads)).reshape(1, lane)

        mod_a = _ada(c, w_ada, b_ada, l, 2 * d).reshape(bsz, 2, d)

        h1 = _prenorm(x2, mod_a, pre_norm_mix[l], seq)
        qkv = _mm(h1, w_in_nk, 3 * attn_w, BF16, 1024, 1024, "in_proj_qkv", w_is_nk=True,
                  scaled_cols=attn_w, scale=HEAD_DIM ** -0.5 * LOG2E)
        y_conv, w_o = _conv_proj(h1, w_gcu_nk, conv_w[l], w_out, l, seq)
        f_logit = _mm(h1, w_f_nk, lane, F32, 1024, lane, "in_proj_forget", w_is_nk=True)
        qaug, kaug = _fcum(f_logit.reshape(bsz, seq, lane), b_pad, n_heads)
        y_attn = _attention(qkv.reshape(bsz, seq, 3 * attn_w), qaug, kaug, n_heads)
        y_cat = _mixnorm(y_attn.reshape(t, attn_w), y_conv,
                         attn_out_norm[l], conv_out_norm[l])
        h, mod_b = _out_proj(y_cat, w_o, c_pad, bsz, w_ada, b_ada, l, 2 * d)
        mod3 = jnp.concatenate([mod_a, mod_b.reshape(bsz, N_MOD - 2, d)], axis=1)
        x2, h2 = _res1(x2, h, mod3, post_norm_mix[l], pre_norm_ffn[l], seq)

        a, w_d = _ffn_up(h2, w_gate, w_up, w_down, l)
        h3 = _mm(a, w_d, d, F32, 512, 512, "swiglu_down")
        x2 = _res2(x2, h3, mod3, post_norm_ffn[l], seq)
    return x2.reshape(bsz, seq, d)
```
